```python
import math
import jax, jax.numpy as jnp
from jax import lax
import numpy as np

D_MODEL = 2048
BATCH = 2
SEQ = 16384
DEPTH = 4

HEAD_DIM = 64
MIX_WIDTH = D_MODEL // 4
BRANCH_W = MIX_WIDTH // 2
N_HEADS = BRANCH_W // HEAD_DIM
N_IN_PARTS = 8
ROPE_THETA = 10000.0
DIL_PAIRS = ((128, 1), (512, 4), (2048, 16))
DIL_STEPS = 128
MOBA_BLOCK = 256
MOBA_TOPK = 3
DIFF_HEAD_DIM = HEAD_DIM // 2
Q_BLOCK = 128
RMS_EPS = 1e-6
N_ODD = DEPTH // 2

kernel_name = "hybrid_dilated_moba_diff_stickbreak"


def _rmsnorm(x, g):
    xf = x.astype(jnp.float32)
    y = xf * lax.rsqrt(jnp.mean(xf * xf, axis=-1, keepdims=True) + RMS_EPS)
    return (y * g.astype(jnp.float32)).astype(x.dtype)


def _rope(t, pos):
    d = t.shape[-1]
    half = d // 2
    inv = ROPE_THETA ** (-jnp.arange(half, dtype=jnp.float32) / half)
    ang = pos.astype(jnp.float32)[:, None] * inv[None, :]
    cos, sin = jnp.cos(ang), jnp.sin(ang)
    tf = t.astype(jnp.float32)
    t1, t2 = tf[..., :half], tf[..., half:]
    return jnp.concatenate([t1 * cos - t2 * sin, t1 * sin + t2 * cos], axis=-1).astype(t.dtype)


def _heads(t):
    b, s, _ = t.shape
    return t.reshape(b, s, N_HEADS, -1).transpose(0, 2, 1, 3)


def _merge(t):
    b, h, s, d = t.shape
    return t.transpose(0, 2, 1, 3).reshape(b, s, h * d)


def _dilated_branch(q, k, v, r):
    b, h, s, d = q.shape
    L = -(-s // (r * DIL_STEPS)) * DIL_STEPS
    pad = L * r - s
    nb = L // DIL_STEPS

    def to_phase(t):
        t = jnp.pad(t, ((0, 0), (0, 0), (0, pad), (0, 0)))
        t = t.reshape(b, h, L, r, d).transpose(0, 1, 3, 2, 4)
        return t.reshape(b, h, r, nb, DIL_STEPS, d)

    qb, kb, vb = to_phase(q), to_phase(k), to_phase(v)
    prev = lambda t: jnp.pad(t, ((0, 0), (0, 0), (0, 0), (1, 0), (0, 0), (0, 0)))[:, :, :, :-1]
    kk = jnp.concatenate([prev(kb), kb], axis=4)
    vv = jnp.concatenate([prev(vb), vb], axis=4)
    sc = jnp.einsum('bhrnqd,bhrnkd->bhrnqk', qb, kk).astype(jnp.float32) / math.sqrt(d)
    i = jnp.arange(DIL_STEPS)[:, None]
    j = jnp.arange(2 * DIL_STEPS)[None, :]
    band = (j >= i) & (j <= i + DIL_STEPS)
    exists = (jnp.arange(nb)[:, None, None] > 0) | (j[None] >= DIL_STEPS)
    mask = band[None] & exists
    sc = jnp.where(mask, sc, -jnp.inf)
    m = jnp.max(sc, axis=-1, keepdims=True)
    p = jnp.exp(sc - m)
    den = jnp.sum(p, axis=-1)
    o = jnp.einsum('bhrnqk,bhrnkd->bhrnqd', p, vv.astype(jnp.float32)) / den[..., None]
    lse = m[..., 0] + jnp.log(den)
    o = o.reshape(b, h, r, L, d).transpose(0, 1, 3, 2, 4).reshape(b, h, L * r, d)[:, :, :s]
    lse = lse.reshape(b, h, r, L).transpose(0, 1, 3, 2).reshape(b, h, L * r)[:, :, :s]
    return o, lse


def _dilated_mixture(q, k, v):
    outs, lses = [], []
    for _, r in DIL_PAIRS:
        o, l = _dilated_branch(q, k, v, r)
        outs.append(o)
        lses.append(l)
    w = jax.nn.softmax(jnp.stack(lses, axis=0), axis=0)[..., None]
    return jnp.sum(w * jnp.stack(outs, axis=0), axis=0).astype(q.dtype)


def _moba(q, k, v):
    b, h, s, d = q.shape
    sp = -(-s // MOBA_BLOCK) * MOBA_BLOCK
    nblk = sp // MOBA_BLOCK
    kp = jnp.pad(k, ((0, 0), (0, 0), (0, sp - s), (0, 0)))
    vp = jnp.pad(v, ((0, 0), (0, 0), (0, sp - s), (0, 0)))
    kb = kp.reshape(b, h, nblk, MOBA_BLOCK, d)
    vb = vp.reshape(b, h, nblk, MOBA_BLOCK, d)
    kmean = jnp.mean(kb.astype(jnp.float32), axis=3)
    n_sel = min(MOBA_TOPK, nblk)
    nch = s // Q_BLOCK
    qc = q.reshape(b, h, nch, Q_BLOCK, d).transpose(2, 0, 1, 3, 4)
    starts = jnp.arange(nch, dtype=jnp.int32) * Q_BLOCK
    b_ix = jnp.arange(b)[:, None, None, None]
    h_ix = jnp.arange(h)[None, :, None, None]
    scale = 1.0 / math.sqrt(d)
    kpos_blk = jnp.arange(MOBA_BLOCK)

    def chunk(args):
        qch, t0 = args
        cur = t0 // MOBA_BLOCK
        gate = jnp.einsum('bhqd,bhnd->bhqn', qch.astype(jnp.float32), kmean)
        gate = jnp.where(jnp.arange(nblk) < cur, gate, -jnp.inf)
        _, sel = lax.top_k(gate, n_sel)
        kg = kb[b_ix, h_ix, sel]
        vg = vb[b_ix, h_ix, sel]
        k_own = lax.dynamic_slice_in_dim(kp, cur * MOBA_BLOCK, MOBA_BLOCK, axis=2)
        v_own = lax.dynamic_slice_in_dim(vp, cur * MOBA_BLOCK, MOBA_BLOCK, axis=2)
        s_sel = jnp.einsum('bhqd,bhqnkd->bhqnk', qch, kg).astype(jnp.float32) * scale
        s_sel = jnp.where((jnp.arange(n_sel) < cur)[:, None], s_sel, -jnp.inf)
        s_own = jnp.einsum('bhqd,bhkd->bhqk', qch, k_own).astype(jnp.float32) * scale
        qpos = t0 + jnp.arange(Q_BLOCK)
        s_own = jnp.where(cur * MOBA_BLOCK + kpos_blk[None, :] <= qpos[:, None], s_own, -jnp.inf)
        n_sk = n_sel * MOBA_BLOCK
        p = jax.nn.softmax(jnp.concatenate([s_sel.reshape(b, h, Q_BLOCK, n_sk), s_own], axis=-1), axis=-1)
        p_sel = p[..., :n_sk].reshape(b, h, Q_BLOCK, n_sel, MOBA_BLOCK).astype(v.dtype)
        p_own = p[..., n_sk:].astype(v.dtype)
        return (jnp.einsum('bhqnk,bhqnkd->bhqd', p_sel, vg)
                + jnp.einsum('bhqk,bhkd->bhqd', p_own, v_own))

    out = lax.map(chunk, (qc, starts))
    return out.transpose(1, 2, 0, 3, 4).reshape(b, h, s, d)


def _diff_attention(q, k, v, lam, subln_g, lambda_init):
    s = k.shape[-2]
    scale = 1.0 / math.sqrt(q.shape[-1])
    lf = lam.astype(jnp.float32)
    lam_val = jnp.exp(jnp.sum(lf[0] * lf[1])) - jnp.exp(jnp.sum(lf[2] * lf[3])) + lambda_init
    outs = []
    for t0 in range(0, s, Q_BLOCK):
        kl = t0 + Q_BLOCK
        sc = jnp.einsum('bhiqd,bhikd->bhiqk', q[:, :, :, t0:kl], k[:, :, :, :kl]).astype(jnp.float32) * scale
        causal = jnp.arange(kl)[None, :] <= (t0 + jnp.arange(Q_BLOCK))[:, None]
        p = jax.nn.softmax(jnp.where(causal, sc, -jnp.inf), axis=-1)
        a = p[:, :, 0] - lam_val * p[:, :, 1]
        outs.append(jnp.einsum('bhqk,bhkd->bhqd', a.astype(v.dtype), v[:, :, :kl]))
    o = _rmsnorm(jnp.concatenate(outs, axis=2), subln_g)
    return (o.astype(jnp.float32) * (1.0 - lambda_init)).astype(v.dtype)


def _stick_breaking(q, k, v):
    b, h, s, d = q.shape
    scale = 1.0 / math.sqrt(d)
    ar = jnp.arange(Q_BLOCK)
    tri = (ar[:, None] > ar[None, :]).astype(jnp.float32)
    outs = []
    for i, t0 in enumerate(range(0, s, Q_BLOCK)):
        nc = i + 1
        kl = nc * Q_BLOCK
        z = jnp.einsum('bhqd,bhkd->bhqk', q[:, :, t0:kl], k[:, :, :kl]).astype(jnp.float32) * scale
        before = jnp.arange(kl)[None, :] < (t0 + ar)[:, None]
        ln1m = jnp.where(before, jax.nn.log_sigmoid(-z), 0.0)
        ln1m_c = ln1m.reshape(b, h, Q_BLOCK, nc, Q_BLOCK)
        tri_c = (jnp.arange(nc)[:, None] > jnp.arange(nc)[None, :]).astype(jnp.float32)
        tail = (jnp.einsum('bhqcj,js->bhqcs', ln1m_c, tri)
                + jnp.einsum('bhqe,ec->bhqc', jnp.sum(ln1m_c, axis=-1), tri_c)[..., None])
        log_a = jnp.where(before, z + ln1m + tail.reshape(b, h, Q_BLOCK, kl), -jnp.inf)
        outs.append(jnp.einsum('bhqk,bhkd->bhqd', jnp.exp(log_a).astype(v.dtype), v[:, :, :kl]))
    return jnp.concatenate(outs, axis=2)


def setup_inputs(seed: int = 0) -> dict:
    key = jax.random.key(seed)
    ks = jax.random.split(key, 8)
    x = jax.random.normal(ks[0], (BATCH, SEQ, D_MODEL), jnp.float32)
    norm_g = 1.0 + 0.02 * jax.random.normal(ks[1], (DEPTH, D_MODEL), jnp.float32)
    w_in = jax.random.normal(ks[2], (DEPTH, D_MODEL, N_IN_PARTS * BRANCH_W), jnp.float32) * D_MODEL ** -0.5
    w_out = jax.random.normal(ks[3], (DEPTH, MIX_WIDTH, D_MODEL), jnp.float32) * MIX_WIDTH ** -0.5
    diff_lam = 0.1 * jax.random.normal(ks[4], (N_ODD, 4, DIFF_HEAD_DIM), jnp.float32)
    diff_subln_g = 1.0 + 0.02 * jax.random.normal(ks[5], (N_ODD, HEAD_DIM), jnp.float32)
    final_norm_g = 1.0 + 0.02 * jax.random.normal(ks[6], (D_MODEL,), jnp.float32)
    return {"x": x, "norm_g": norm_g, "w_in": w_in, "w_out": w_out,
            "diff_lam": diff_lam, "diff_subln_g": diff_subln_g,
            "final_norm_g": final_norm_g}


def reference(x, norm_g, w_in, w_out, diff_lam, diff_subln_g, final_norm_g):
    b, s, _ = x.shape
    pos = jnp.arange(s)
    for layer in range(DEPTH):
        h = _rmsnorm(x, norm_g[layer])
        proj = jnp.einsum('bsd,de->bse', h, w_in[layer])
        parts = jnp.split(proj, N_IN_PARTS, axis=-1)
        if layer % 2 == 0:
            qa, ka, va, ga, qb, kb, vb, gb = parts
            oa = _dilated_mixture(_rope(_heads(qa), pos), _rope(_heads(ka), pos), _heads(va))
            ob = _moba(_rope(_heads(qb), pos), _rope(_heads(kb), pos), _heads(vb))
            y = jnp.concatenate([_merge(oa) * jax.nn.silu(ga),
                                 _merge(ob) * jax.nn.silu(gb)], axis=-1)
        else:
            qc, kc, vc, gc, qd, kd, vd, gd = parts
            li = layer // 2
            lambda_init = 0.8 - 0.6 * math.exp(-0.3 * layer)
            split2 = lambda t: t.reshape(b, s, N_HEADS, 2, DIFF_HEAD_DIM).transpose(0, 2, 3, 1, 4)
            oc = _diff_attention(_rope(split2(qc), pos), _rope(split2(kc), pos), _heads(vc),
                                 diff_lam[li], diff_subln_g[li], lambda_init)
            od = _stick_breaking(_heads(qd), _heads(kd), _heads(vd))
            y = jnp.concatenate([_merge(oc) * jax.nn.silu(gc),
                                 _merge(od) * jax.nn.silu(gd)], axis=-1)
        x = x + jnp.einsum('bse,ed->bsd', y.astype(x.dtype), w_out[layer])
    return _rmsnorm(x, final_norm_g)
```

```python
import functools
import math

import jax
import jax.numpy as jnp
from jax import lax
from jax.experimental import pallas as pl
from jax.experimental.pallas import tpu as pltpu

HEAD_DIM = 64
N_HEADS = 4
BRANCH_W = N_HEADS * HEAD_DIM
N_IN_PARTS = 8
PROJ_W = N_IN_PARTS * BRANCH_W
LANES = 128
HEADS_PER_BLOCK = LANES // HEAD_DIM
BLOCKS_PER_PART = BRANCH_W // LANES
ROPE_THETA = 10000.0
DIL_RATES = (1, 4, 16)
DIL_STEPS = 128
DIL_SUPER = DIL_STEPS * max(DIL_RATES)
MOBA_BLOCK = 256
MOBA_TOPK = 3
DIFF_HEAD_DIM = HEAD_DIM // 2
RMS_EPS = 1e-6
NEG = -1e30

ATT_TQ = 256
ATT_TK = 256
PROJ_TM = 512
VMEM_LIMIT = 56 * 1024 * 1024

_NT = (((1,), (1,)), ((), ()))


def _nt_dot(a, b, precision=None):
    return lax.dot_general(a, b, _NT, preferred_element_type=jnp.float32, precision=precision)


def _lane_iota(shape):
    return lax.broadcasted_iota(jnp.int32, shape, 1)


def _row_iota(shape):
    return lax.broadcasted_iota(jnp.int32, shape, 0)


def _silu(g):
    return g * (1.0 / (1.0 + jnp.exp(-g)))


def _rep(a, n):
    return a if n == 1 else jnp.concatenate([a] * n, axis=1)


def _inproj_kernel(x_ref, g_ref, w_ref, cos_ref, sin_ref, pf_ref, pb_ref, *rest, even):
    x = x_ref[...]
    ms = jnp.mean(x * x, axis=-1, keepdims=True)
    h = (x * lax.rsqrt(ms + RMS_EPS)) * g_ref[...]
    proj = jnp.dot(h.astype(jnp.bfloat16), w_ref[...], preferred_element_type=jnp.float32)

    if even:
        rope_parts, shift = (0, 1, 4, 5), HEAD_DIM // 2
        scales = {0: 1.0 / math.sqrt(HEAD_DIM), 4: 1.0 / math.sqrt(HEAD_DIM)}
    else:
        rope_parts, shift = (0, 1), DIFF_HEAD_DIM // 2
        scales = {0: 1.0 / math.sqrt(DIFF_HEAD_DIM), 4: 1.0 / math.sqrt(HEAD_DIM)}
    cos = cos_ref[...]
    sin = sin_ref[...]
    first_half = (_lane_iota(cos.shape) % (2 * shift)) < shift

    for c in range(PROJ_W // LANES):
        part = c // BLOCKS_PER_PART
        t = proj[:, c * LANES:(c + 1) * LANES]
        if part in rope_parts:
            up = pltpu.roll(t, LANES - shift, axis=1)
            down = pltpu.roll(t, shift, axis=1)
            t = t * cos + jnp.where(first_half, up, down) * sin
        if part in scales:
            t = t * scales[part]
        pf_ref[:, c * LANES:(c + 1) * LANES] = t
        pb_ref[:, c * LANES:(c + 1) * LANES] = t.astype(jnp.bfloat16)
        if even and part == 5:
            km_ref = rest[0]
            cb = c - 5 * BLOCKS_PER_PART
            nblk = t.shape[0] // MOBA_BLOCK
            km_ref[:, cb * LANES:(cb + 1) * LANES] = jnp.mean(
                t.reshape(nblk, MOBA_BLOCK, LANES), axis=1)


def _inproj(x, g, w_b, layer, cos, sin, even):
    b, s, d = x.shape
    tm = min(PROJ_TM, s)
    assert s % tm == 0 and tm % MOBA_BLOCK == 0
    ni = s // tm
    out_shape = [jax.ShapeDtypeStruct((b, s, PROJ_W), jnp.float32),
                 jax.ShapeDtypeStruct((b, s, PROJ_W), jnp.bfloat16)]
    out_specs = [pl.BlockSpec((None, tm, PROJ_W), lambda bi, i: (bi, i, 0)),
                 pl.BlockSpec((None, tm, PROJ_W), lambda bi, i: (bi, i, 0))]
    if even:
        out_shape.append(jax.ShapeDtypeStruct((b, ni, tm // MOBA_BLOCK, BRANCH_W), jnp.float32))
        out_specs.append(pl.BlockSpec((None, None, tm // MOBA_BLOCK, BRANCH_W),
                                      lambda bi, i: (bi, i, 0, 0)))
    return pl.pallas_call(
        functools.partial(_inproj_kernel, even=even),
        grid=(b, ni),
        in_specs=[
            pl.BlockSpec((None, tm, d), lambda bi, i: (bi, i, 0)),
            pl.BlockSpec((None, 1, d), lambda bi, i: (layer, 0, 0)),
            pl.BlockSpec((None, d, PROJ_W), lambda bi, i: (layer, 0, 0)),
            pl.BlockSpec((tm, LANES), lambda bi, i: (i, 0)),
            pl.BlockSpec((tm, LANES), lambda bi, i: (i, 0)),
        ],
        out_specs=out_specs,
        out_shape=out_shape,
        compiler_params=pltpu.CompilerParams(
            dimension_semantics=("parallel", "parallel"), vmem_limit_bytes=VMEM_LIMIT),
        name="inproj_even" if even else "inproj_odd",
    )(x, g, w_b, cos, sin)


def _outproj_kernel(x_ref, y1_ref, y2_ref, w_ref, fg_ref, o_ref, *, final):
    y = jnp.concatenate([y1_ref[...], y2_ref[...]], axis=1)
    xn = x_ref[...] + jnp.dot(y, w_ref[...], preferred_element_type=jnp.float32)
    if final:
        ms = jnp.mean(xn * xn, axis=-1, keepdims=True)
        xn = (xn * lax.rsqrt(ms + RMS_EPS)) * fg_ref[...]
    o_ref[...] = xn


def _outproj(x, y1, y2, w_b, layer, fg, final):
    b, s, d = x.shape
    tm = min(PROJ_TM, s)
    return pl.pallas_call(
        functools.partial(_outproj_kernel, final=final),
        grid=(b, s // tm),
        in_specs=[
            pl.BlockSpec((None, tm, d), lambda bi, i: (bi, i, 0)),
            pl.BlockSpec((None, tm, BRANCH_W), lambda bi, i: (bi, i, 0)),
            pl.BlockSpec((None, tm, BRANCH_W), lambda bi, i: (bi, i, 0)),
            pl.BlockSpec((None, 2 * BRANCH_W, d), lambda bi, i: (layer, 0, 0)),
            pl.BlockSpec((1, d), lambda bi, i: (0, 0)),
        ],
        out_specs=pl.BlockSpec((None, tm, d), lambda bi, i: (bi, i, 0)),
        out_shape=jax.ShapeDtypeStruct((b, s, d), jnp.float32),
        compiler_params=pltpu.CompilerParams(
            dimension_semantics=("parallel", "parallel"), vmem_limit_bytes=VMEM_LIMIT),
        name="outproj_final" if final else "outproj",
    )(x, y1, y2, w_b, fg)


def _dilated_kernel(q_ref, kc_ref, kp_ref, vc_ref, vp_ref, g_ref, o_ref,
                    kwin, vwin, m_sc, l_sc, acc_sc):
    sb = q_ref.shape[0]
    first_super = pl.program_id(2) == 0
    kwin[0:sb, :] = kp_ref[...]
    kwin[sb:2 * sb, :] = kc_ref[...]
    vwin[0:sb, :] = vp_ref[...]
    vwin[sb:2 * sb, :] = vc_ref[...]

    t = DIL_STEPS
    lane = _lane_iota((t, LANES))
    row = _row_iota((t, 2 * t))
    col = _lane_iota((t, 2 * t))
    band = (col >= row) & (col <= row + t)

    for ri, r in enumerate(DIL_RATES):
        n_tiles = sb // t

        def tile(j, carry, r=r, ri=ri):
            phase = j % r
            nb = j // r
            q0 = phase + r * t * nb
            rows = pl.ds(q0, t, stride=r) if r > 1 else pl.ds(q0, t)
            own = pl.ds(sb + q0, t, stride=r) if r > 1 else pl.ds(sb + q0, t)
            prev = pl.ds(sb + q0 - r * t, t, stride=r) if r > 1 else pl.ds(sb + q0 - r * t, t)
            qt = q_ref[rows, :]
            kk = jnp.concatenate([kwin[prev, :], kwin[own, :]], axis=0).astype(jnp.bfloat16)
            vv = jnp.concatenate([vwin[prev, :], vwin[own, :]], axis=0).astype(jnp.bfloat16)
            no_prev = jnp.logical_and(first_super, nb == 0)
            mask = band & (col >= jnp.where(no_prev, t, 0))
            ms, ls, accs = [], [], []
            for h in range(HEADS_PER_BLOCK):
                hm = (lane // HEAD_DIM) == h
                qh = jnp.where(hm, qt, 0.0).astype(jnp.bfloat16)
                s = jnp.where(mask, _nt_dot(qh, kk), NEG)
                m = jnp.max(s, axis=1, keepdims=True)
                p = jnp.exp(s - m)
                ms.append(m)
                ls.append(jnp.sum(p, axis=1, keepdims=True))
                accs.append(jnp.dot(p.astype(jnp.bfloat16), vv, preferred_element_type=jnp.float32))
            lo = lane < HEAD_DIM
            m_sc[ri, rows, :] = jnp.where(lo, ms[0], ms[1])
            l_sc[ri, rows, :] = jnp.where(lo, ls[0], ls[1])
            acc_sc[ri, rows, :] = jnp.where(lo, accs[0], accs[1])
            return carry

        lax.fori_loop(0, n_tiles, tile, 0)

    chunk = 256
    def combine(c, carry):
        rows = pl.ds(pl.multiple_of(c * chunk, chunk), chunk)
        m0, m1, m2 = m_sc[0, rows, :], m_sc[1, rows, :], m_sc[2, rows, :]
        mx = jnp.maximum(jnp.maximum(m0, m1), m2)
        w0, w1, w2 = jnp.exp(m0 - mx), jnp.exp(m1 - mx), jnp.exp(m2 - mx)
        num = w0 * acc_sc[0, rows, :] + w1 * acc_sc[1, rows, :] + w2 * acc_sc[2, rows, :]
        den = w0 * l_sc[0, rows, :] + w1 * l_sc[1, rows, :] + w2 * l_sc[2, rows, :]
        o_ref[rows, :] = ((num / den) * _silu(g_ref[rows, :])).astype(o_ref.dtype)
        return carry
    lax.fori_loop(0, sb // chunk, combine, 0)


def _dilated(pf):
    b, s, _ = pf.shape
    sb = DIL_SUPER
    assert s % sb == 0
    blk = lambda part, prev: pl.BlockSpec(
        (None, sb, LANES),
        (lambda bi, hp, i: (bi, jnp.maximum(i - 1, 0), part * BLOCKS_PER_PART + hp)) if prev
        else (lambda bi, hp, i: (bi, i, part * BLOCKS_PER_PART + hp)))
    return pl.pallas_call(
        _dilated_kernel,
        grid=(b, BLOCKS_PER_PART, s // sb),
        in_specs=[blk(0, False), blk(1, False), blk(1, True), blk(2, False), blk(2, True),
                  blk(3, False)],
        out_specs=pl.BlockSpec((None, sb, LANES), lambda bi, hp, i: (bi, i, hp)),
        out_shape=jax.ShapeDtypeStruct((b, s, BRANCH_W), jnp.bfloat16),
        scratch_shapes=[
            pltpu.VMEM((2 * sb, LANES), jnp.float32),
            pltpu.VMEM((2 * sb, LANES), jnp.float32),
            pltpu.VMEM((len(DIL_RATES), sb, LANES), jnp.float32),
            pltpu.VMEM((len(DIL_RATES), sb, LANES), jnp.float32),
            pltpu.VMEM((len(DIL_RATES), sb, LANES), jnp.float32),
        ],
        compiler_params=pltpu.CompilerParams(
            dimension_semantics=("parallel", "parallel", "arbitrary"),
            vmem_limit_bytes=VMEM_LIMIT),
        name="dilated",
    )(pf, pf, pf, pf, pf, pf)


def _softmax_init(s, v, m_sc, l_sc, acc_sc, idx):
    m = jnp.max(s, axis=1, keepdims=True)
    p = jnp.exp(s - m)
    m_sc[idx] = jnp.broadcast_to(m, m_sc.shape[1:])
    l_sc[idx] = jnp.broadcast_to(jnp.sum(p, axis=1, keepdims=True), l_sc.shape[1:])
    acc_sc[idx] = jnp.dot(p.astype(jnp.bfloat16), v, preferred_element_type=jnp.float32)


def _softmax_update(s, v, m_sc, l_sc, acc_sc, idx):
    m_prev = m_sc[idx]
    m_new = jnp.maximum(m_prev, jnp.max(s, axis=1, keepdims=True))
    alpha = jnp.exp(m_prev - m_new)
    p = jnp.exp(s - _rep(m_new, s.shape[1] // LANES))
    m_sc[idx] = m_new
    l_sc[idx] = alpha * l_sc[idx] + jnp.sum(p, axis=1, keepdims=True)
    acc_sc[idx] = alpha * acc_sc[idx] + jnp.dot(
        p.astype(jnp.bfloat16), v, preferred_element_type=jnp.float32)


def _moba_kernel(q_ref, k_ref, v_ref, e_ref, km_ref, g_ref, o_ref,
                 qa_sc, m_sc, l_sc, acc_sc):
    tq = q_ref.shape[0]
    cur = pl.program_id(2)
    lane = _lane_iota((tq, LANES))
    q = q_ref[...]
    km = km_ref[...]
    for h in range(HEADS_PER_BLOCK):
        hm = (lane // HEAD_DIM) == h
        qh = jnp.where(hm, q, jnp.zeros_like(q))
        gate = _nt_dot(qh.astype(jnp.float32), km, precision=lax.Precision.HIGHEST)
        avail = lane < cur
        gate = jnp.where(avail, gate, -jnp.inf)
        sel = lane == cur
        for _ in range(MOBA_TOPK):
            mx = jnp.max(gate, axis=1, keepdims=True)
            first = jnp.min(jnp.where(gate == mx, lane, LANES), axis=1, keepdims=True)
            pick = (lane == first) & (mx > -jnp.inf)
            sel = sel | pick
            gate = jnp.where(pick, -jnp.inf, gate)
        bias = jnp.where(sel, 0.0, NEG).astype(jnp.bfloat16)
        qa_sc[h] = jnp.concatenate([qh, bias], axis=1)

    def kv(n):
        rows = pl.ds(pl.multiple_of(n * MOBA_BLOCK, MOBA_BLOCK), MOBA_BLOCK)
        return jnp.concatenate([k_ref[rows, :], e_ref[rows, :]], axis=1), v_ref[rows, :]

    ka, vb = kv(cur)
    causal = _lane_iota((tq, MOBA_BLOCK)) <= _row_iota((tq, MOBA_BLOCK))
    for h in range(HEADS_PER_BLOCK):
        s = jnp.where(causal, _nt_dot(qa_sc[h], ka), NEG)
        _softmax_init(s, vb, m_sc, l_sc, acc_sc, h)

    def body(n, carry):
        ka, vb = kv(n)
        for h in range(HEADS_PER_BLOCK):
            _softmax_update(_nt_dot(qa_sc[h], ka), vb, m_sc, l_sc, acc_sc, h)
        return carry
    lax.fori_loop(0, cur, body, 0)

    o = jnp.where(lane < HEAD_DIM, acc_sc[0] / l_sc[0], acc_sc[1] / l_sc[1])
    o_ref[...] = (o * _silu(g_ref[...])).astype(o_ref.dtype)


def _moba(pb, pf, km, e):
    b, s, _ = pb.shape
    tq = MOBA_BLOCK
    assert s % tq == 0 and s // MOBA_BLOCK <= LANES
    col = lambda part: (lambda bi, hp, i: (bi, i, part * BLOCKS_PER_PART + hp))
    whole = lambda part: (lambda bi, hp, i: (bi, 0, part * BLOCKS_PER_PART + hp))
    return pl.pallas_call(
        _moba_kernel,
        grid=(b, BLOCKS_PER_PART, s // tq),
        in_specs=[
            pl.BlockSpec((None, tq, LANES), col(4)),
            pl.BlockSpec((None, s, LANES), whole(5)),
            pl.BlockSpec((None, s, LANES), whole(6)),
            pl.BlockSpec((s, LANES), lambda bi, hp, i: (0, 0)),
            pl.BlockSpec((None, LANES, LANES), lambda bi, hp, i: (bi, 0, hp)),
            pl.BlockSpec((None, tq, LANES), col(7)),
        ],
        out_specs=pl.BlockSpec((None, tq, LANES), lambda bi, hp, i: (bi, i, hp)),
        out_shape=jax.ShapeDtypeStruct((b, s, BRANCH_W), jnp.bfloat16),
        scratch_shapes=[
            pltpu.VMEM((HEADS_PER_BLOCK, tq, 2 * LANES), jnp.bfloat16),
            pltpu.VMEM((HEADS_PER_BLOCK, tq, LANES), jnp.float32),
            pltpu.VMEM((HEADS_PER_BLOCK, tq, LANES), jnp.float32),
            pltpu.VMEM((HEADS_PER_BLOCK, tq, LANES), jnp.float32),
        ],
        compiler_params=pltpu.CompilerParams(
            dimension_semantics=("parallel", "parallel", "arbitrary"),
            vmem_limit_bytes=VMEM_LIMIT),
        name="moba",
    )(pb, pb, pb, e, km, pf)


def _diff_kernel(q_ref, k_ref, v_ref, lam_ref, sg_ref, g_ref, o_ref,
                 qm_sc, m_sc, l_sc, acc_sc, *, lambda_init):
    tq = q_ref.shape[0]
    i = pl.program_id(2)
    lane = _lane_iota((tq, LANES))
    q = q_ref[...]
    n_maps = LANES // DIFF_HEAD_DIM
    for c in range(n_maps):
        qm_sc[c] = jnp.where((lane // DIFF_HEAD_DIM) == c, q, jnp.zeros_like(q))

    def kv(n):
        rows = pl.ds(pl.multiple_of(n * ATT_TK, ATT_TK), ATT_TK)
        return k_ref[rows, :], v_ref[rows, :]

    kb, vb = kv(i)
    causal = _lane_iota((tq, ATT_TK)) <= _row_iota((tq, ATT_TK))
    for c in range(n_maps):
        s = jnp.where(causal, _nt_dot(qm_sc[c], kb), NEG)
        _softmax_init(s, vb, m_sc, l_sc, acc_sc, c)

    def body(n, carry):
        kb, vb = kv(n)
        for c in range(n_maps):
            _softmax_update(_nt_dot(qm_sc[c], kb), vb, m_sc, l_sc, acc_sc, c)
        return carry
    lax.fori_loop(0, i, body, 0)

    lf = lam_ref[...]
    lam = (jnp.exp(jnp.sum(lf[0:1] * lf[1:2], axis=1, keepdims=True))
           - jnp.exp(jnp.sum(lf[2:3] * lf[3:4], axis=1, keepdims=True)) + lambda_init)
    o0 = acc_sc[0] / l_sc[0] - lam * (acc_sc[1] / l_sc[1])
    o1 = acc_sc[2] / l_sc[2] - lam * (acc_sc[3] / l_sc[3])
    lo = lane < HEAD_DIM
    o = jnp.where(lo, o0, o1)
    sq = o * o
    ms0 = jnp.sum(jnp.where(lo, sq, 0.0), axis=1, keepdims=True) / HEAD_DIM
    ms1 = jnp.sum(jnp.where(lo, 0.0, sq), axis=1, keepdims=True) / HEAD_DIM
    y = (o * lax.rsqrt(jnp.where(lo, ms0, ms1) + RMS_EPS)) * sg_ref[...]
    y = y * (1.0 - lambda_init)
    o_ref[...] = (y * _silu(g_ref[...])).astype(o_ref.dtype)


def _diff(pb, pf, lam, sg, lambda_init):
    b, s, _ = pb.shape
    tq = ATT_TQ
    assert tq == ATT_TK and s % tq == 0
    col = lambda part: (lambda bi, hp, i: (bi, i, part * BLOCKS_PER_PART + hp))
    whole = lambda part: (lambda bi, hp, i: (bi, 0, part * BLOCKS_PER_PART + hp))
    n_maps = LANES // DIFF_HEAD_DIM
    return pl.pallas_call(
        functools.partial(_diff_kernel, lambda_init=lambda_init),
        grid=(b, BLOCKS_PER_PART, s // tq),
        in_specs=[
            pl.BlockSpec((None, tq, LANES), col(0)),
            pl.BlockSpec((None, s, LANES), whole(1)),
            pl.BlockSpec((None, s, LANES), whole(2)),
            pl.BlockSpec((4, LANES), lambda bi, hp, i: (0, 0)),
            pl.BlockSpec((1, LANES), lambda bi, hp, i: (0, 0)),
            pl.BlockSpec((None, tq, LANES), col(3)),
        ],
        out_specs=pl.BlockSpec((None, tq, LANES), lambda bi, hp, i: (bi, i, hp)),
        out_shape=jax.ShapeDtypeStruct((b, s, BRANCH_W), jnp.bfloat16),
        scratch_shapes=[
            pltpu.VMEM((n_maps, tq, LANES), jnp.bfloat16),
            pltpu.VMEM((n_maps, tq, LANES), jnp.float32),
            pltpu.VMEM((n_maps, tq, LANES), jnp.float32),
            pltpu.VMEM((n_maps, tq, LANES), jnp.float32),
        ],
        compiler_params=pltpu.CompilerParams(
            dimension_semantics=("parallel", "parallel", "arbitrary"),
            vmem_limit_bytes=VMEM_LIMIT),
        name="diff",
    )(pb, pb, pb, lam, sg, pf)


def _stick_kernel(q_ref, k_ref, v_ref, tri_ref, g_ref, o_ref, qh_sc, carry_sc, acc_sc):
    tq = q_ref.shape[0]
    i = pl.program_id(2)
    lane = _lane_iota((tq, LANES))
    q = q_ref[...]
    for h in range(HEADS_PER_BLOCK):
        qh_sc[h] = jnp.where((lane // HEAD_DIM) == h, q, jnp.zeros_like(q))
    tri = tri_ref[...]

    def step(n, h, before):
        rows = pl.ds(pl.multiple_of(n * ATT_TK, ATT_TK), ATT_TK)
        z = _nt_dot(qh_sc[h], k_ref[rows, :])
        ln1m = -(jnp.maximum(z, 0.0) + jnp.log1p(jnp.exp(-jnp.abs(z))))
        if before is not None:
            ln1m = jnp.where(before, ln1m, 0.0)
        tail = jnp.dot(ln1m.astype(jnp.bfloat16), tri, preferred_element_type=jnp.float32)
        log_a = z + ln1m + tail
        if before is None:
            log_a = log_a + _rep(carry_sc[h], ATT_TK // LANES)
        else:
            log_a = jnp.where(before, log_a, NEG)
        pv = jnp.dot(jnp.exp(log_a).astype(jnp.bfloat16), v_ref[rows, :],
                     preferred_element_type=jnp.float32)
        rowsum = jnp.sum(ln1m, axis=1, keepdims=True)
        if before is None:
            acc_sc[h] = acc_sc[h] + pv
            carry_sc[h] = carry_sc[h] + rowsum
        else:
            acc_sc[h] = pv
            carry_sc[h] = jnp.broadcast_to(rowsum, carry_sc.shape[1:])

    strictly_before = _lane_iota((tq, ATT_TK)) < _row_iota((tq, ATT_TK))
    for h in range(HEADS_PER_BLOCK):
        step(i, h, strictly_before)

    def body(t, carry):
        for h in range(HEADS_PER_BLOCK):
            step(i - 1 - t, h, None)
        return carry
    lax.fori_loop(0, i, body, 0)

    o = jnp.where(lane < HEAD_DIM, acc_sc[0], acc_sc[1])
    o_ref[...] = (o * _silu(g_ref[...])).astype(o_ref.dtype)


def _stick(pb, pf, tri):
    b, s, _ = pb.shape
    tq = ATT_TQ
    assert tq == ATT_TK and s % tq == 0
    col = lambda part: (lambda bi, hp, i: (bi, i, part * BLOCKS_PER_PART + hp))
    whole = lambda part: (lambda bi, hp, i: (bi, 0, part * BLOCKS_PER_PART + hp))
    return pl.pallas_call(
        _stick_kernel,
        grid=(b, BLOCKS_PER_PART, s // tq),
        in_specs=[
            pl.BlockSpec((None, tq, LANES), col(4)),
            pl.BlockSpec((None, s, LANES), whole(5)),
            pl.BlockSpec((None, s, LANES), whole(6)),
            pl.BlockSpec((ATT_TK, ATT_TK), lambda bi, hp, i: (0, 0)),
            pl.BlockSpec((None, tq, LANES), col(7)),
        ],
        out_specs=pl.BlockSpec((None, tq, LANES), lambda bi, hp, i: (bi, i, hp)),
        out_shape=jax.ShapeDtypeStruct((b, s, BRANCH_W), jnp.bfloat16),
        scratch_shapes=[
            pltpu.VMEM((HEADS_PER_BLOCK, tq, LANES), jnp.bfloat16),
            pltpu.VMEM((HEADS_PER_BLOCK, tq, LANES), jnp.float32),
            pltpu.VMEM((HEADS_PER_BLOCK, tq, LANES), jnp.float32),
        ],
        compiler_params=pltpu.CompilerParams(
            dimension_semantics=("parallel", "parallel", "arbitrary"),
            vmem_limit_bytes=VMEM_LIMIT),
        name="stick",
    )(pb, pb, pb, tri, pf)


def _rope_tables(s, dim):
    half = dim // 2
    inv = ROPE_THETA ** (-jnp.arange(half, dtype=jnp.float32) / half)
    ang = jnp.arange(s).astype(jnp.float32)[:, None] * inv[None, :]
    cos = jnp.tile(jnp.cos(ang), (1, LANES // half))
    sin = jnp.tile(jnp.sin(ang), (1, LANES // half))
    sign = jnp.where((jnp.arange(LANES) % dim) < half, -1.0, 1.0).astype(jnp.float32)
    return cos, sin * sign[None, :]


def kernel(x, norm_g, w_in, w_out, diff_lam, diff_subln_g, final_norm_g):
    b, s, d = x.shape
    depth = w_in.shape[0]
    w_in_b = w_in.astype(jnp.bfloat16)
    w_out_b = w_out.astype(jnp.bfloat16)
    norm_g3 = norm_g.reshape(depth, 1, d)
    fg = final_norm_g.reshape(1, d)
    rope_even = _rope_tables(s, HEAD_DIM)
    rope_odd = _rope_tables(s, DIFF_HEAD_DIM)
    n_moba = s // MOBA_BLOCK
    block_code = (jnp.arange(s)[:, None] // MOBA_BLOCK == jnp.arange(LANES)[None, :]
                  ).astype(jnp.bfloat16)
    tri = (jnp.arange(ATT_TK)[:, None] > jnp.arange(ATT_TK)[None, :]).astype(jnp.bfloat16)

    for layer in range(depth):
        even = layer % 2 == 0
        if even:
            pf, pb, km = _inproj(x, norm_g3, w_in_b, layer, *rope_even, even=True)
            km = km.reshape(b, n_moba, BRANCH_W)
            km = jnp.pad(km, ((0, 0), (0, LANES - n_moba), (0, 0)))
            y1 = _dilated(pf)
            y2 = _moba(pb, pf, km, block_code)
        else:
            pf, pb = _inproj(x, norm_g3, w_in_b, layer, *rope_odd, even=False)
            li = layer // 2
            lambda_init = 0.8 - 0.6 * math.exp(-0.3 * layer)
            lam = jnp.pad(diff_lam[li], ((0, 0), (0, LANES - DIFF_HEAD_DIM)))
            sg = jnp.tile(diff_subln_g[li], HEADS_PER_BLOCK).reshape(1, LANES)
            y1 = _diff(pb, pf, lam, sg, lambda_init)
            y2 = _stick(pb, pf, tri)
        x = _outproj(x, y1, y2, w_out_b, layer, fg, final=(layer == depth - 1))
    return x
```

```python
import functools
import math

import jax
import jax.numpy as jnp
from jax import lax
from jax.experimental import pallas as pl
from jax.experimental.pallas import tpu as pltpu

HEAD_DIM = 64
N_HEADS = 4
BRANCH_W = N_HEADS * HEAD_DIM
N_IN_PARTS = 8
PROJ_W = N_IN_PARTS * BRANCH_W
LANES = 128
HEADS_PER_BLOCK = LANES // HEAD_DIM
BLOCKS_PER_PART = BRANCH_W // LANES
ROPE_THETA = 10000.0
DIL_RATES = (1, 4, 16)
DIL_STEPS = 128
DIL_SUPER = DIL_STEPS * max(DIL_RATES)
MOBA_BLOCK = 256
MOBA_TOPK = 3
DIFF_HEAD_DIM = HEAD_DIM // 2
RMS_EPS = 1e-6
NEG = -1e30
LOG2E = math.log2(math.e)

ATT_TQ = 512
ATT_TK = 512
TRI_BLOCK = 256
PROJ_TM = 512
VMEM_LIMIT = 56 * 1024 * 1024

_NT = (((1,), (1,)), ((), ()))


def _nt_dot(a, b, precision=None):
    return lax.dot_general(a, b, _NT, preferred_element_type=jnp.float32, precision=precision)


def _lane_iota(shape):
    return lax.broadcasted_iota(jnp.int32, shape, 1)


def _row_iota(shape):
    return lax.broadcasted_iota(jnp.int32, shape, 0)


def _silu(g):
    return g * (1.0 / (1.0 + jnp.exp(-g)))


def _rep(a, n):
    return a if n == 1 else jnp.concatenate([a] * n, axis=1)


def _inproj_kernel(x_ref, g_ref, w_ref, cos_ref, sin_ref, pf_ref, pb_ref, *rest, even):
    x = x_ref[...]
    ms = jnp.mean(x * x, axis=-1, keepdims=True)
    h = (x * lax.rsqrt(ms + RMS_EPS)) * g_ref[...]
    proj = jnp.dot(h.astype(jnp.bfloat16), w_ref[...], preferred_element_type=jnp.float32)

    if even:
        rope_parts, shift = (0, 1, 4, 5), HEAD_DIM // 2
        scales = {0: LOG2E / math.sqrt(HEAD_DIM), 4: LOG2E / math.sqrt(HEAD_DIM)}
    else:
        rope_parts, shift = (0, 1), DIFF_HEAD_DIM // 2
        scales = {0: LOG2E / math.sqrt(DIFF_HEAD_DIM), 4: LOG2E / math.sqrt(HEAD_DIM)}
    cos = cos_ref[...]
    sin = sin_ref[...]
    first_half = (_lane_iota(cos.shape) % (2 * shift)) < shift

    for c in range(PROJ_W // LANES):
        part = c // BLOCKS_PER_PART
        t = proj[:, c * LANES:(c + 1) * LANES]
        if part in rope_parts:
            up = pltpu.roll(t, LANES - shift, axis=1)
            down = pltpu.roll(t, shift, axis=1)
            t = t * cos + jnp.where(first_half, up, down) * sin
        if part in scales:
            t = t * scales[part]
        pf_ref[:, c * LANES:(c + 1) * LANES] = t
        pb_ref[:, c * LANES:(c + 1) * LANES] = t.astype(jnp.bfloat16)
        if even and part == 5:
            km_ref = rest[0]
            cb = c - 5 * BLOCKS_PER_PART
            nblk = t.shape[0] // MOBA_BLOCK
            km_ref[:, cb * LANES:(cb + 1) * LANES] = jnp.mean(
                t.reshape(nblk, MOBA_BLOCK, LANES), axis=1)


def _inproj(x, g, w_b, layer, cos, sin, even):
    b, s, d = x.shape
    tm = min(PROJ_TM, s)
    assert s % tm == 0 and tm % MOBA_BLOCK == 0
    ni = s // tm
    out_shape = [jax.ShapeDtypeStruct((b, s, PROJ_W), jnp.float32),
                 jax.ShapeDtypeStruct((b, s, PROJ_W), jnp.bfloat16)]
    out_specs = [pl.BlockSpec((None, tm, PROJ_W), lambda bi, i: (bi, i, 0)),
                 pl.BlockSpec((None, tm, PROJ_W), lambda bi, i: (bi, i, 0))]
    if even:
        out_shape.append(jax.ShapeDtypeStruct((b, ni, tm // MOBA_BLOCK, BRANCH_W), jnp.float32))
        out_specs.append(pl.BlockSpec((None, None, tm // MOBA_BLOCK, BRANCH_W),
                                      lambda bi, i: (bi, i, 0, 0)))
    return pl.pallas_call(
        functools.partial(_inproj_kernel, even=even),
        grid=(b, ni),
        in_specs=[
            pl.BlockSpec((None, tm, d), lambda bi, i: (bi, i, 0)),
            pl.BlockSpec((None, 1, d), lambda bi, i: (layer, 0, 0)),
            pl.BlockSpec((None, d, PROJ_W), lambda bi, i: (layer, 0, 0)),
            pl.BlockSpec((tm, LANES), lambda bi, i: (i, 0)),
            pl.BlockSpec((tm, LANES), lambda bi, i: (i, 0)),
        ],
        out_specs=out_specs,
        out_shape=out_shape,
        compiler_params=pltpu.CompilerParams(
            dimension_semantics=("parallel", "parallel"), vmem_limit_bytes=VMEM_LIMIT),
        name="inproj_even" if even else "inproj_odd",
    )(x, g, w_b, cos, sin)


def _outproj_kernel(x_ref, y1_ref, y2_ref, w_ref, fg_ref, o_ref, *, final):
    y = jnp.concatenate([y1_ref[...], y2_ref[...]], axis=1)
    xn = x_ref[...] + jnp.dot(y, w_ref[...], preferred_element_type=jnp.float32)
    if final:
        ms = jnp.mean(xn * xn, axis=-1, keepdims=True)
        xn = (xn * lax.rsqrt(ms + RMS_EPS)) * fg_ref[...]
    o_ref[...] = xn


def _outproj(x, y1, y2, w_b, layer, fg, final):
    b, s, d = x.shape
    tm = min(PROJ_TM, s)
    return pl.pallas_call(
        functools.partial(_outproj_kernel, final=final),
        grid=(b, s // tm),
        in_specs=[
            pl.BlockSpec((None, tm, d), lambda bi, i: (bi, i, 0)),
            pl.BlockSpec((None, tm, BRANCH_W), lambda bi, i: (bi, i, 0)),
            pl.BlockSpec((None, tm, BRANCH_W), lambda bi, i: (bi, i, 0)),
            pl.BlockSpec((None, 2 * BRANCH_W, d), lambda bi, i: (layer, 0, 0)),
            pl.BlockSpec((1, d), lambda bi, i: (0, 0)),
        ],
        out_specs=pl.BlockSpec((None, tm, d), lambda bi, i: (bi, i, 0)),
        out_shape=jax.ShapeDtypeStruct((b, s, d), jnp.float32),
        compiler_params=pltpu.CompilerParams(
            dimension_semantics=("parallel", "parallel"), vmem_limit_bytes=VMEM_LIMIT),
        name="outproj_final" if final else "outproj",
    )(x, y1, y2, w_b, fg)


def _dilated_kernel(q_ref, kc_ref, kp_ref, vc_ref, vp_ref, g_ref, o_ref,
                    kwin, vwin, m_sc, l_sc, acc_sc):
    sb = q_ref.shape[0]
    first_super = pl.program_id(2) == 0
    kwin[0:sb, :] = kp_ref[...]
    kwin[sb:2 * sb, :] = kc_ref[...]
    vwin[0:sb, :] = vp_ref[...]
    vwin[sb:2 * sb, :] = vc_ref[...]

    t = DIL_STEPS
    lane = _lane_iota((t, LANES))
    row = _row_iota((t, 2 * t))
    col = _lane_iota((t, 2 * t))
    band = (col >= row) & (col <= row + t)

    for ri, r in enumerate(DIL_RATES):
        n_tiles = sb // t

        def tile(j, carry, r=r, ri=ri):
            phase = j % r
            nb = j // r
            q0 = phase + r * t * nb
            rows = pl.ds(q0, t, stride=r) if r > 1 else pl.ds(q0, t)
            own = pl.ds(sb + q0, t, stride=r) if r > 1 else pl.ds(sb + q0, t)
            prev = pl.ds(sb + q0 - r * t, t, stride=r) if r > 1 else pl.ds(sb + q0 - r * t, t)
            qt = q_ref[rows, :]
            kk = jnp.concatenate([kwin[prev, :], kwin[own, :]], axis=0).astype(jnp.bfloat16)
            vv = jnp.concatenate([vwin[prev, :], vwin[own, :]], axis=0).astype(jnp.bfloat16)
            no_prev = jnp.logical_and(first_super, nb == 0)
            mask = band & (col >= jnp.where(no_prev, t, 0))
            ms, ls, accs = [], [], []
            for h in range(HEADS_PER_BLOCK):
                hm = (lane // HEAD_DIM) == h
                qh = jnp.where(hm, qt, 0.0).astype(jnp.bfloat16)
                s = jnp.where(mask, _nt_dot(qh, kk), NEG)
                m = jnp.max(s, axis=1, keepdims=True)
                p = jnp.exp2(s - m)
                ms.append(m)
                ls.append(jnp.sum(p, axis=1, keepdims=True))
                accs.append(jnp.dot(p.astype(jnp.bfloat16), vv, preferred_element_type=jnp.float32))
            lo = lane < HEAD_DIM
            m_sc[ri, rows, :] = jnp.where(lo, ms[0], ms[1])
            l_sc[ri, rows, :] = jnp.where(lo, ls[0], ls[1])
            acc_sc[ri, rows, :] = jnp.where(lo, accs[0], accs[1])
            return carry

        lax.fori_loop(0, n_tiles, tile, 0, unroll=8)

    chunk = 256
    def combine(c, carry):
        rows = pl.ds(pl.multiple_of(c * chunk, chunk), chunk)
        m0, m1, m2 = m_sc[0, rows, :], m_sc[1, rows, :], m_sc[2, rows, :]
        mx = jnp.maximum(jnp.maximum(m0, m1), m2)
        w0, w1, w2 = jnp.exp2(m0 - mx), jnp.exp2(m1 - mx), jnp.exp2(m2 - mx)
        num = w0 * acc_sc[0, rows, :] + w1 * acc_sc[1, rows, :] + w2 * acc_sc[2, rows, :]
        den = w0 * l_sc[0, rows, :] + w1 * l_sc[1, rows, :] + w2 * l_sc[2, rows, :]
        o_ref[rows, :] = ((num / den) * _silu(g_ref[rows, :])).astype(o_ref.dtype)
        return carry
    lax.fori_loop(0, sb // chunk, combine, 0)


def _dilated(pf):
    b, s, _ = pf.shape
    sb = DIL_SUPER
    assert s % sb == 0
    blk = lambda part, prev: pl.BlockSpec(
        (None, sb, LANES),
        (lambda bi, hp, i: (bi, jnp.maximum(i - 1, 0), part * BLOCKS_PER_PART + hp)) if prev
        else (lambda bi, hp, i: (bi, i, part * BLOCKS_PER_PART + hp)))
    return pl.pallas_call(
        _dilated_kernel,
        grid=(b, BLOCKS_PER_PART, s // sb),
        in_specs=[blk(0, False), blk(1, False), blk(1, True), blk(2, False), blk(2, True),
                  blk(3, False)],
        out_specs=pl.BlockSpec((None, sb, LANES), lambda bi, hp, i: (bi, i, hp)),
        out_shape=jax.ShapeDtypeStruct((b, s, BRANCH_W), jnp.bfloat16),
        scratch_shapes=[
            pltpu.VMEM((2 * sb, LANES), jnp.float32),
            pltpu.VMEM((2 * sb, LANES), jnp.float32),
            pltpu.VMEM((len(DIL_RATES), sb, LANES), jnp.float32),
            pltpu.VMEM((len(DIL_RATES), sb, LANES), jnp.float32),
            pltpu.VMEM((len(DIL_RATES), sb, LANES), jnp.float32),
        ],
        compiler_params=pltpu.CompilerParams(
            dimension_semantics=("parallel", "parallel", "arbitrary"),
            vmem_limit_bytes=VMEM_LIMIT),
        name="dilated",
    )(pf, pf, pf, pf, pf, pf)


def _softmax_init(s, v, m_sc, l_sc, acc_sc, idx):
    m = jnp.max(s, axis=1, keepdims=True)
    p = jnp.exp2(s - m)
    m_sc[idx] = jnp.broadcast_to(m, m_sc.shape[1:])
    l_sc[idx] = jnp.broadcast_to(jnp.sum(p, axis=1, keepdims=True), l_sc.shape[1:])
    acc_sc[idx] = jnp.dot(p.astype(jnp.bfloat16), v, preferred_element_type=jnp.float32)


def _softmax_update(s, v, m_sc, l_sc, acc_sc, idx):
    m_prev = m_sc[idx]
    m_new = jnp.maximum(m_prev, jnp.max(s, axis=1, keepdims=True))
    alpha = jnp.exp2(m_prev - m_new)
    p = jnp.exp2(s - _rep(m_new, s.shape[1] // LANES))
    m_sc[idx] = m_new
    l_sc[idx] = alpha * l_sc[idx] + jnp.sum(p, axis=1, keepdims=True)
    acc_sc[idx] = alpha * acc_sc[idx] + jnp.dot(
        p.astype(jnp.bfloat16), v, preferred_element_type=jnp.float32)


def _moba_kernel(q_ref, k_ref, v_ref, e_ref, km_ref, g_ref, o_ref,
                 qa_sc, m_sc, l_sc, acc_sc):
    tq = q_ref.shape[0]
    i = pl.program_id(2)
    lane = _lane_iota((tq, LANES))
    cur = i * (tq // MOBA_BLOCK) + _row_iota((tq, LANES)) // MOBA_BLOCK
    q = q_ref[...]
    km = km_ref[...]
    for h in range(HEADS_PER_BLOCK):
        hm = (lane // HEAD_DIM) == h
        qh = jnp.where(hm, q, jnp.zeros_like(q))
        gate = _nt_dot(qh.astype(jnp.float32), km, precision=lax.Precision.HIGHEST)
        avail = lane < cur
        gate = jnp.where(avail, gate, -jnp.inf)
        sel = lane == cur
        for _ in range(MOBA_TOPK):
            mx = jnp.max(gate, axis=1, keepdims=True)
            first = jnp.min(jnp.where(gate == mx, lane, LANES), axis=1, keepdims=True)
            pick = (lane == first) & (mx > -jnp.inf)
            sel = sel | pick
            gate = jnp.where(pick, -jnp.inf, gate)
        bias = jnp.where(sel, 0.0, NEG).astype(jnp.bfloat16)
        qa_sc[h] = jnp.concatenate([qh, bias], axis=1)

    def kv(n):
        rows = pl.ds(pl.multiple_of(n * tq, tq), tq)
        return jnp.concatenate([k_ref[rows, :], e_ref[rows, :]], axis=1), v_ref[rows, :]

    ka, vb = kv(i)
    causal = _lane_iota((tq, tq)) <= _row_iota((tq, tq))
    for h in range(HEADS_PER_BLOCK):
        s = jnp.where(causal, _nt_dot(qa_sc[h], ka), NEG)
        _softmax_init(s, vb, m_sc, l_sc, acc_sc, h)

    def body(n, carry):
        ka, vb = kv(n)
        for h in range(HEADS_PER_BLOCK):
            _softmax_update(_nt_dot(qa_sc[h], ka), vb, m_sc, l_sc, acc_sc, h)
        return carry
    lax.fori_loop(0, i, body, 0)

    o = jnp.where(lane < HEAD_DIM, acc_sc[0] / l_sc[0], acc_sc[1] / l_sc[1])
    o_ref[...] = (o * _silu(g_ref[...])).astype(o_ref.dtype)


def _moba(pb, pf, km, e):
    b, s, _ = pb.shape
    tq = min(ATT_TQ, s)
    assert s % tq == 0 and tq % MOBA_BLOCK == 0 and s // MOBA_BLOCK <= LANES
    col = lambda part: (lambda bi, hp, i: (bi, i, part * BLOCKS_PER_PART + hp))
    whole = lambda part: (lambda bi, hp, i: (bi, 0, part * BLOCKS_PER_PART + hp))
    return pl.pallas_call(
        _moba_kernel,
        grid=(b, BLOCKS_PER_PART, s // tq),
        in_specs=[
            pl.BlockSpec((None, tq, LANES), col(4)),
            pl.BlockSpec((None, s, LANES), whole(5)),
            pl.BlockSpec((None, s, LANES), whole(6)),
            pl.BlockSpec((s, LANES), lambda bi, hp, i: (0, 0)),
            pl.BlockSpec((None, LANES, LANES), lambda bi, hp, i: (bi, 0, hp)),
            pl.BlockSpec((None, tq, LANES), col(7)),
        ],
        out_specs=pl.BlockSpec((None, tq, LANES), lambda bi, hp, i: (bi, i, hp)),
        out_shape=jax.ShapeDtypeStruct((b, s, BRANCH_W), jnp.bfloat16),
        scratch_shapes=[
            pltpu.VMEM((HEADS_PER_BLOCK, tq, 2 * LANES), jnp.bfloat16),
            pltpu.VMEM((HEADS_PER_BLOCK, tq, LANES), jnp.float32),
            pltpu.VMEM((HEADS_PER_BLOCK, tq, LANES), jnp.float32),
            pltpu.VMEM((HEADS_PER_BLOCK, tq, LANES), jnp.float32),
        ],
        compiler_params=pltpu.CompilerParams(
            dimension_semantics=("parallel", "parallel", "arbitrary"),
            vmem_limit_bytes=VMEM_LIMIT),
        name="moba",
    )(pb, pb, pb, e, km, pf)


def _diff_kernel(q_ref, k_ref, v_ref, lam_ref, sg_ref, g_ref, o_ref,
                 qm_sc, m_sc, l_sc, acc_sc, *, lambda_init):
    tq = q_ref.shape[0]
    i = pl.program_id(2)
    lane = _lane_iota((tq, LANES))
    q = q_ref[...]
    n_maps = LANES // DIFF_HEAD_DIM
    for c in range(n_maps):
        qm_sc[c] = jnp.where((lane // DIFF_HEAD_DIM) == c, q, jnp.zeros_like(q))

    def kv(n):
        rows = pl.ds(pl.multiple_of(n * ATT_TK, ATT_TK), ATT_TK)
        return k_ref[rows, :], v_ref[rows, :]

    kb, vb = kv(i)
    causal = _lane_iota((tq, ATT_TK)) <= _row_iota((tq, ATT_TK))
    for c in range(n_maps):
        s = jnp.where(causal, _nt_dot(qm_sc[c], kb), NEG)
        _softmax_init(s, vb, m_sc, l_sc, acc_sc, c)

    def body(n, carry):
        kb, vb = kv(n)
        for c in range(n_maps):
            _softmax_update(_nt_dot(qm_sc[c], kb), vb, m_sc, l_sc, acc_sc, c)
        return carry
    lax.fori_loop(0, i, body, 0)

    lf = lam_ref[...]
    lam = (jnp.exp(jnp.sum(lf[0:1] * lf[1:2], axis=1, keepdims=True))
           - jnp.exp(jnp.sum(lf[2:3] * lf[3:4], axis=1, keepdims=True)) + lambda_init)
    o0 = acc_sc[0] / l_sc[0] - lam * (acc_sc[1] / l_sc[1])
    o1 = acc_sc[2] / l_sc[2] - lam * (acc_sc[3] / l_sc[3])
    lo = lane < HEAD_DIM
    o = jnp.where(lo, o0, o1)
    sq = o * o
    ms0 = jnp.sum(jnp.where(lo, sq, 0.0), axis=1, keepdims=True) / HEAD_DIM
    ms1 = jnp.sum(jnp.where(lo, 0.0, sq), axis=1, keepdims=True) / HEAD_DIM
    y = (o * lax.rsqrt(jnp.where(lo, ms0, ms1) + RMS_EPS)) * sg_ref[...]
    y = y * (1.0 - lambda_init)
    o_ref[...] = (y * _silu(g_ref[...])).astype(o_ref.dtype)


def _diff(pb, pf, lam, sg, lambda_init):
    b, s, _ = pb.shape
    tq = ATT_TQ
    assert tq == ATT_TK and s % tq == 0
    col = lambda part: (lambda bi, hp, i: (bi, i, part * BLOCKS_PER_PART + hp))
    whole = lambda part: (lambda bi, hp, i: (bi, 0, part * BLOCKS_PER_PART + hp))
    n_maps = LANES // DIFF_HEAD_DIM
    return pl.pallas_call(
        functools.partial(_diff_kernel, lambda_init=lambda_init),
        grid=(b, BLOCKS_PER_PART, s // tq),
        in_specs=[
            pl.BlockSpec((None, tq, LANES), col(0)),
            pl.BlockSpec((None, s, LANES), whole(1)),
            pl.BlockSpec((None, s, LANES), whole(2)),
            pl.BlockSpec((4, LANES), lambda bi, hp, i: (0, 0)),
            pl.BlockSpec((1, LANES), lambda bi, hp, i: (0, 0)),
            pl.BlockSpec((None, tq, LANES), col(3)),
        ],
        out_specs=pl.BlockSpec((None, tq, LANES), lambda bi, hp, i: (bi, i, hp)),
        out_shape=jax.ShapeDtypeStruct((b, s, BRANCH_W), jnp.bfloat16),
        scratch_shapes=[
            pltpu.VMEM((n_maps, tq, LANES), jnp.bfloat16),
            pltpu.VMEM((n_maps, tq, LANES), jnp.float32),
            pltpu.VMEM((n_maps, tq, LANES), jnp.float32),
            pltpu.VMEM((n_maps, tq, LANES), jnp.float32),
        ],
        compiler_params=pltpu.CompilerParams(
            dimension_semantics=("parallel", "parallel", "arbitrary"),
            vmem_limit_bytes=VMEM_LIMIT),
        name="diff",
    )(pb, pb, pb, lam, sg, pf)


def _stick_kernel(q_ref, k_ref, v_ref, tri_ref, g_ref, o_ref, qh_sc, carry_sc, acc_sc):
    tq = q_ref.shape[0]
    i = pl.program_id(2)
    lane = _lane_iota((tq, LANES))
    q = q_ref[...]
    for h in range(HEADS_PER_BLOCK):
        qh_sc[h] = jnp.where((lane // HEAD_DIM) == h, q, jnp.zeros_like(q))
    tri = tri_ref[...]

    n_sub = ATT_TK // TRI_BLOCK

    def step(n, h, before):
        rows = pl.ds(pl.multiple_of(n * ATT_TK, ATT_TK), ATT_TK)
        z = _nt_dot(qh_sc[h], k_ref[rows, :])
        ln1m = -(jnp.maximum(z, 0.0) + jnp.log2(1.0 + jnp.exp2(-jnp.abs(z))))
        if before is not None:
            ln1m = jnp.where(before, ln1m, 0.0)
        run = None if before is not None else carry_sc[h]
        a_parts = [None] * n_sub
        for j in reversed(range(n_sub)):
            cols = slice(j * TRI_BLOCK, (j + 1) * TRI_BLOCK)
            lj = ln1m[:, cols]
            log_a = z[:, cols] + lj + jnp.dot(
                lj.astype(jnp.bfloat16), tri, preferred_element_type=jnp.float32)
            if run is not None:
                log_a = log_a + _rep(run, TRI_BLOCK // LANES)
            if before is not None:
                log_a = jnp.where(before[:, cols], log_a, NEG)
            a_parts[j] = jnp.exp2(log_a).astype(jnp.bfloat16)
            rowsum = jnp.sum(lj, axis=1, keepdims=True)
            run = jnp.broadcast_to(rowsum, carry_sc.shape[1:]) if run is None else run + rowsum
        pv = jnp.dot(jnp.concatenate(a_parts, axis=1), v_ref[rows, :],
                     preferred_element_type=jnp.float32)
        carry_sc[h] = run
        acc_sc[h] = pv if before is not None else acc_sc[h] + pv

    strictly_before = _lane_iota((tq, ATT_TK)) < _row_iota((tq, ATT_TK))
    for h in range(HEADS_PER_BLOCK):
        step(i, h, strictly_before)

    def body(t, carry):
        for h in range(HEADS_PER_BLOCK):
            step(i - 1 - t, h, None)
        return carry
    lax.fori_loop(0, i, body, 0)

    o = jnp.where(lane < HEAD_DIM, acc_sc[0], acc_sc[1])
    o_ref[...] = (o * _silu(g_ref[...])).astype(o_ref.dtype)


def _stick(pb, pf, tri):
    b, s, _ = pb.shape
    tq = ATT_TQ
    assert tq == ATT_TK and s % tq == 0
    col = lambda part: (lambda bi, hp, i: (bi, i, part * BLOCKS_PER_PART + hp))
    whole = lambda part: (lambda bi, hp, i: (bi, 0, part * BLOCKS_PER_PART + hp))
    return pl.pallas_call(
        _stick_kernel,
        grid=(b, BLOCKS_PER_PART, s // tq),
        in_specs=[
            pl.BlockSpec((None, tq, LANES), col(4)),
            pl.BlockSpec((None, s, LANES), whole(5)),
            pl.BlockSpec((None, s, LANES), whole(6)),
            pl.BlockSpec((TRI_BLOCK, TRI_BLOCK), lambda bi, hp, i: (0, 0)),
            pl.BlockSpec((None, tq, LANES), col(7)),
        ],
        out_specs=pl.BlockSpec((None, tq, LANES), lambda bi, hp, i: (bi, i, hp)),
        out_shape=jax.ShapeDtypeStruct((b, s, BRANCH_W), jnp.bfloat16),
        scratch_shapes=[
            pltpu.VMEM((HEADS_PER_BLOCK, tq, LANES), jnp.bfloat16),
            pltpu.VMEM((HEADS_PER_BLOCK, tq, LANES), jnp.float32),
            pltpu.VMEM((HEADS_PER_BLOCK, tq, LANES), jnp.float32),
        ],
        compiler_params=pltpu.CompilerParams(
            dimension_semantics=("parallel", "parallel", "arbitrary"),
            vmem_limit_bytes=VMEM_LIMIT),
        name="stick",
    )(pb, pb, pb, tri, pf)


def _rope_tables(s, dim):
    half = dim // 2
    inv = ROPE_THETA ** (-jnp.arange(half, dtype=jnp.float32) / half)
    ang = jnp.arange(s).astype(jnp.float32)[:, None] * inv[None, :]
    cos = jnp.tile(jnp.cos(ang), (1, LANES // half))
    sin = jnp.tile(jnp.sin(ang), (1, LANES // half))
    sign = jnp.where((jnp.arange(LANES) % dim) < half, -1.0, 1.0).astype(jnp.float32)
    return cos, sin * sign[None, :]


def kernel(x, norm_g, w_in, w_out, diff_lam, diff_subln_g, final_norm_g):
    b, s, d = x.shape
    depth = w_in.shape[0]
    w_in_b = w_in.astype(jnp.bfloat16)
    w_out_b = w_out.astype(jnp.bfloat16)
    norm_g3 = norm_g.reshape(depth, 1, d)
    fg = final_norm_g.reshape(1, d)
    rope_even = _rope_tables(s, HEAD_DIM)
    rope_odd = _rope_tables(s, DIFF_HEAD_DIM)
    n_moba = s // MOBA_BLOCK
    block_code = (jnp.arange(s)[:, None] // MOBA_BLOCK == jnp.arange(LANES)[None, :]
                  ).astype(jnp.bfloat16)
    tri = (jnp.arange(TRI_BLOCK)[:, None] > jnp.arange(TRI_BLOCK)[None, :]).astype(jnp.bfloat16)

    for layer in range(depth):
        even = layer % 2 == 0
        if even:
            pf, pb, km = _inproj(x, norm_g3, w_in_b, layer, *rope_even, even=True)
            km = km.reshape(b, n_moba, BRANCH_W)
            km = jnp.pad(km, ((0, 0), (0, LANES - n_moba), (0, 0)))
            y1 = _dilated(pf)
            y2 = _moba(pb, pf, km, block_code)
        else:
            pf, pb = _inproj(x, norm_g3, w_in_b, layer, *rope_odd, even=False)
            li = layer // 2
            lambda_init = 0.8 - 0.6 * math.exp(-0.3 * layer)
            lam = jnp.pad(diff_lam[li], ((0, 0), (0, LANES - DIFF_HEAD_DIM)))
            sg = jnp.tile(diff_subln_g[li], HEADS_PER_BLOCK).reshape(1, LANES)
            y1 = _diff(pb, pf, lam, sg, lambda_init)
            y2 = _stick(pb, pf, tri)
        x = _outproj(x, y1, y2, w_out_b, layer, fg, final=(layer == depth - 1))
    return x
```

```python
import functools
import math

import jax
import jax.numpy as jnp
from jax import lax
from jax.experimental import pallas as pl
from jax.experimental.pallas import tpu as pltpu

HEAD_DIM = 64
N_HEADS = 4
BRANCH_W = N_HEADS * HEAD_DIM
N_IN_PARTS = 8
PROJ_W = N_IN_PARTS * BRANCH_W
LANES = 128
HEADS_PER_BLOCK = LANES // HEAD_DIM
BLOCKS_PER_PART = BRANCH_W // LANES
ROPE_THETA = 10000.0
DIL_RATES = (1, 4, 16)
DIL_STEPS = 128
DIL_SUPER = DIL_STEPS * max(DIL_RATES)
MOBA_BLOCK = 256
MOBA_TOPK = 3
DIFF_HEAD_DIM = HEAD_DIM // 2
RMS_EPS = 1e-6
NEG = -1e30
LOG2E = math.log2(math.e)

ATT_TQ = 1024
ATT_TK = 1024
STICK_T = 512
TRI_BLOCK = 256
PROJ_TM = 512
VMEM_LIMIT = 56 * 1024 * 1024

_NT = (((1,), (1,)), ((), ()))


def _nt_dot(a, b, precision=None):
    return lax.dot_general(a, b, _NT, preferred_element_type=jnp.float32, precision=precision)


def _lane_iota(shape):
    return lax.broadcasted_iota(jnp.int32, shape, 1)


def _row_iota(shape):
    return lax.broadcasted_iota(jnp.int32, shape, 0)


def _silu(g):
    return g * (1.0 / (1.0 + jnp.exp(-g)))


def _rep(a, n):
    return a if n == 1 else jnp.concatenate([a] * n, axis=1)


def _inproj_kernel(x_ref, g_ref, w_ref, cos_ref, sin_ref, pf_ref, pb_ref, *rest, even):
    x = x_ref[...]
    ms = jnp.mean(x * x, axis=-1, keepdims=True)
    h = (x * lax.rsqrt(ms + RMS_EPS)) * g_ref[...]
    proj = jnp.dot(h.astype(jnp.bfloat16), w_ref[...], preferred_element_type=jnp.float32)

    if even:
        rope_parts, shift = (0, 1, 4, 5), HEAD_DIM // 2
        scales = {0: LOG2E / math.sqrt(HEAD_DIM), 4: LOG2E / math.sqrt(HEAD_DIM)}
    else:
        rope_parts, shift = (0, 1), DIFF_HEAD_DIM // 2
        scales = {0: LOG2E / math.sqrt(DIFF_HEAD_DIM), 4: LOG2E / math.sqrt(HEAD_DIM)}
    cos = cos_ref[...]
    sin = sin_ref[...]
    first_half = (_lane_iota(cos.shape) % (2 * shift)) < shift

    for c in range(PROJ_W // LANES):
        part = c // BLOCKS_PER_PART
        t = proj[:, c * LANES:(c + 1) * LANES]
        if part in rope_parts:
            up = pltpu.roll(t, LANES - shift, axis=1)
            down = pltpu.roll(t, shift, axis=1)
            t = t * cos + jnp.where(first_half, up, down) * sin
        if part in scales:
            t = t * scales[part]
        pf_ref[:, c * LANES:(c + 1) * LANES] = t
        pb_ref[:, c * LANES:(c + 1) * LANES] = t.astype(jnp.bfloat16)
        if even and part == 5:
            km_ref = rest[0]
            cb = c - 5 * BLOCKS_PER_PART
            nblk = t.shape[0] // MOBA_BLOCK
            km_ref[:, cb * LANES:(cb + 1) * LANES] = jnp.mean(
                t.reshape(nblk, MOBA_BLOCK, LANES), axis=1)


def _inproj(x, g, w_b, layer, cos, sin, even):
    b, s, d = x.shape
    tm = min(PROJ_TM, s)
    assert s % tm == 0 and tm % MOBA_BLOCK == 0
    ni = s // tm
    out_shape = [jax.ShapeDtypeStruct((b, s, PROJ_W), jnp.float32),
                 jax.ShapeDtypeStruct((b, s, PROJ_W), jnp.bfloat16)]
    out_specs = [pl.BlockSpec((None, tm, PROJ_W), lambda bi, i: (bi, i, 0)),
                 pl.BlockSpec((None, tm, PROJ_W), lambda bi, i: (bi, i, 0))]
    if even:
        out_shape.append(jax.ShapeDtypeStruct((b, ni, tm // MOBA_BLOCK, BRANCH_W), jnp.float32))
        out_specs.append(pl.BlockSpec((None, None, tm // MOBA_BLOCK, BRANCH_W),
                                      lambda bi, i: (bi, i, 0, 0)))
    return pl.pallas_call(
        functools.partial(_inproj_kernel, even=even),
        grid=(b, ni),
        in_specs=[
            pl.BlockSpec((None, tm, d), lambda bi, i: (bi, i, 0)),
            pl.BlockSpec((None, 1, d), lambda bi, i: (layer, 0, 0)),
            pl.BlockSpec((None, d, PROJ_W), lambda bi, i: (layer, 0, 0)),
            pl.BlockSpec((tm, LANES), lambda bi, i: (i, 0)),
            pl.BlockSpec((tm, LANES), lambda bi, i: (i, 0)),
        ],
        out_specs=out_specs,
        out_shape=out_shape,
        compiler_params=pltpu.CompilerParams(
            dimension_semantics=("parallel", "parallel"), vmem_limit_bytes=VMEM_LIMIT),
        name="inproj_even" if even else "inproj_odd",
    )(x, g, w_b, cos, sin)


def _outproj_kernel(x_ref, y1_ref, y2_ref, w_ref, fg_ref, o_ref, *, final):
    y = jnp.concatenate([y1_ref[...], y2_ref[...]], axis=1)
    xn = x_ref[...] + jnp.dot(y, w_ref[...], preferred_element_type=jnp.float32)
    if final:
        ms = jnp.mean(xn * xn, axis=-1, keepdims=True)
        xn = (xn * lax.rsqrt(ms + RMS_EPS)) * fg_ref[...]
    o_ref[...] = xn


def _outproj(x, y1, y2, w_b, layer, fg, final):
    b, s, d = x.shape
    tm = min(PROJ_TM, s)
    return pl.pallas_call(
        functools.partial(_outproj_kernel, final=final),
        grid=(b, s // tm),
        in_specs=[
            pl.BlockSpec((None, tm, d), lambda bi, i: (bi, i, 0)),
            pl.BlockSpec((None, tm, BRANCH_W), lambda bi, i: (bi, i, 0)),
            pl.BlockSpec((None, tm, BRANCH_W), lambda bi, i: (bi, i, 0)),
            pl.BlockSpec((None, 2 * BRANCH_W, d), lambda bi, i: (layer, 0, 0)),
            pl.BlockSpec((1, d), lambda bi, i: (0, 0)),
        ],
        out_specs=pl.BlockSpec((None, tm, d), lambda bi, i: (bi, i, 0)),
        out_shape=jax.ShapeDtypeStruct((b, s, d), jnp.float32),
        compiler_params=pltpu.CompilerParams(
            dimension_semantics=("parallel", "parallel"), vmem_limit_bytes=VMEM_LIMIT),
        name="outproj_final" if final else "outproj",
    )(x, y1, y2, w_b, fg)


def _dilated_kernel(q_ref, kc_ref, kp_ref, vc_ref, vp_ref, g_ref, o_ref,
                    kwin, vwin, m_sc, l_sc, acc_sc):
    sb = q_ref.shape[0]
    first_super = pl.program_id(2) == 0
    kwin[0:sb, :] = kp_ref[...]
    kwin[sb:2 * sb, :] = kc_ref[...]
    vwin[0:sb, :] = vp_ref[...]
    vwin[sb:2 * sb, :] = vc_ref[...]

    t = DIL_STEPS
    lane = _lane_iota((t, LANES))
    row = _row_iota((t, 2 * t))
    col = _lane_iota((t, 2 * t))
    band = (col >= row) & (col <= row + t)

    for ri, r in enumerate(DIL_RATES):
        n_tiles = sb // t

        def tile(j, carry, r=r, ri=ri):
            phase = j % r
            nb = j // r
            q0 = phase + r * t * nb
            rows = pl.ds(q0, t, stride=r) if r > 1 else pl.ds(q0, t)
            own = pl.ds(sb + q0, t, stride=r) if r > 1 else pl.ds(sb + q0, t)
            prev = pl.ds(sb + q0 - r * t, t, stride=r) if r > 1 else pl.ds(sb + q0 - r * t, t)
            qt = q_ref[rows, :]
            kk = jnp.concatenate([kwin[prev, :], kwin[own, :]], axis=0).astype(jnp.bfloat16)
            vv = jnp.concatenate([vwin[prev, :], vwin[own, :]], axis=0).astype(jnp.bfloat16)
            no_prev = jnp.logical_and(first_super, nb == 0)
            mask = band & (col >= jnp.where(no_prev, t, 0))
            ms, ls, accs = [], [], []
            for h in range(HEADS_PER_BLOCK):
                hm = (lane // HEAD_DIM) == h
                qh = jnp.where(hm, qt, 0.0).astype(jnp.bfloat16)
                s = jnp.where(mask, _nt_dot(qh, kk), NEG)
                m = jnp.max(s, axis=1, keepdims=True)
                p = jnp.exp2(s - m)
                ms.append(m)
                ls.append(jnp.sum(p, axis=1, keepdims=True))
                accs.append(jnp.dot(p.astype(jnp.bfloat16), vv, preferred_element_type=jnp.float32))
            lo = lane < HEAD_DIM
            m_sc[ri, rows, :] = jnp.where(lo, ms[0], ms[1])
            l_sc[ri, rows, :] = jnp.where(lo, ls[0], ls[1])
            acc_sc[ri, rows, :] = jnp.where(lo, accs[0], accs[1])
            return carry

        lax.fori_loop(0, n_tiles, tile, 0, unroll=8)

    chunk = 256
    def combine(c, carry):
        rows = pl.ds(pl.multiple_of(c * chunk, chunk), chunk)
        m0, m1, m2 = m_sc[0, rows, :], m_sc[1, rows, :], m_sc[2, rows, :]
        mx = jnp.maximum(jnp.maximum(m0, m1), m2)
        w0, w1, w2 = jnp.exp2(m0 - mx), jnp.exp2(m1 - mx), jnp.exp2(m2 - mx)
        num = w0 * acc_sc[0, rows, :] + w1 * acc_sc[1, rows, :] + w2 * acc_sc[2, rows, :]
        den = w0 * l_sc[0, rows, :] + w1 * l_sc[1, rows, :] + w2 * l_sc[2, rows, :]
        o_ref[rows, :] = ((num / den) * _silu(g_ref[rows, :])).astype(o_ref.dtype)
        return carry
    lax.fori_loop(0, sb // chunk, combine, 0)


def _dilated(pf):
    b, s, _ = pf.shape
    sb = DIL_SUPER
    assert s % sb == 0
    blk = lambda part, prev: pl.BlockSpec(
        (None, sb, LANES),
        (lambda bi, hp, i: (bi, jnp.maximum(i - 1, 0), part * BLOCKS_PER_PART + hp)) if prev
        else (lambda bi, hp, i: (bi, i, part * BLOCKS_PER_PART + hp)))
    return pl.pallas_call(
        _dilated_kernel,
        grid=(b, BLOCKS_PER_PART, s // sb),
        in_specs=[blk(0, False), blk(1, False), blk(1, True), blk(2, False), blk(2, True),
                  blk(3, False)],
        out_specs=pl.BlockSpec((None, sb, LANES), lambda bi, hp, i: (bi, i, hp)),
        out_shape=jax.ShapeDtypeStruct((b, s, BRANCH_W), jnp.bfloat16),
        scratch_shapes=[
            pltpu.VMEM((2 * sb, LANES), jnp.float32),
            pltpu.VMEM((2 * sb, LANES), jnp.float32),
            pltpu.VMEM((len(DIL_RATES), sb, LANES), jnp.float32),
            pltpu.VMEM((len(DIL_RATES), sb, LANES), jnp.float32),
            pltpu.VMEM((len(DIL_RATES), sb, LANES), jnp.float32),
        ],
        compiler_params=pltpu.CompilerParams(
            dimension_semantics=("parallel", "parallel", "arbitrary"),
            vmem_limit_bytes=VMEM_LIMIT),
        name="dilated",
    )(pf, pf, pf, pf, pf, pf)


def _softmax_init(s, v, m_sc, l_sc, acc_sc, idx):
    m = jnp.max(s, axis=1, keepdims=True)
    p = jnp.exp2(s - m)
    m_sc[idx] = jnp.broadcast_to(m, m_sc.shape[1:])
    l_sc[idx] = jnp.broadcast_to(jnp.sum(p, axis=1, keepdims=True), l_sc.shape[1:])
    acc_sc[idx] = jnp.dot(p.astype(jnp.bfloat16), v, preferred_element_type=jnp.float32)


def _softmax_update(s, v, m_sc, l_sc, acc_sc, idx):
    m_prev = m_sc[idx]
    m_new = jnp.maximum(m_prev, jnp.max(s, axis=1, keepdims=True))
    alpha = jnp.exp2(m_prev - m_new)
    p = jnp.exp2(s - _rep(m_new, s.shape[1] // LANES))
    m_sc[idx] = m_new
    l_sc[idx] = alpha * l_sc[idx] + jnp.sum(p, axis=1, keepdims=True)
    acc_sc[idx] = alpha * acc_sc[idx] + jnp.dot(
        p.astype(jnp.bfloat16), v, preferred_element_type=jnp.float32)


def _moba_kernel(q_ref, k_ref, v_ref, e_ref, km_ref, g_ref, o_ref,
                 qa_sc, m_sc, l_sc, acc_sc):
    tq = q_ref.shape[0]
    i = pl.program_id(2)
    lane = _lane_iota((tq, LANES))
    cur = i * (tq // MOBA_BLOCK) + _row_iota((tq, LANES)) // MOBA_BLOCK
    q = q_ref[...]
    km = km_ref[...]
    for h in range(HEADS_PER_BLOCK):
        hm = (lane // HEAD_DIM) == h
        qh = jnp.where(hm, q, jnp.zeros_like(q))
        gate = _nt_dot(qh.astype(jnp.float32), km, precision=lax.Precision.HIGHEST)
        avail = lane < cur
        gate = jnp.where(avail, gate, -jnp.inf)
        sel = lane == cur
        for _ in range(MOBA_TOPK):
            mx = jnp.max(gate, axis=1, keepdims=True)
            first = jnp.min(jnp.where(gate == mx, lane, LANES), axis=1, keepdims=True)
            pick = (lane == first) & (mx > -jnp.inf)
            sel = sel | pick
            gate = jnp.where(pick, -jnp.inf, gate)
        bias = jnp.where(sel, 0.0, NEG).astype(jnp.bfloat16)
        qa_sc[h] = jnp.concatenate([qh, bias], axis=1)

    def kv(n):
        rows = pl.ds(pl.multiple_of(n * tq, tq), tq)
        return jnp.concatenate([k_ref[rows, :], e_ref[rows, :]], axis=1), v_ref[rows, :]

    ka, vb = kv(i)
    causal = _lane_iota((tq, tq)) <= _row_iota((tq, tq))
    for h in range(HEADS_PER_BLOCK):
        s = jnp.where(causal, _nt_dot(qa_sc[h], ka), NEG)
        _softmax_init(s, vb, m_sc, l_sc, acc_sc, h)

    def body(n, carry):
        ka, vb = kv(n)
        for h in range(HEADS_PER_BLOCK):
            _softmax_update(_nt_dot(qa_sc[h], ka), vb, m_sc, l_sc, acc_sc, h)
        return carry
    lax.fori_loop(0, i, body, 0)

    o = jnp.where(lane < HEAD_DIM, acc_sc[0] / l_sc[0], acc_sc[1] / l_sc[1])
    o_ref[...] = (o * _silu(g_ref[...])).astype(o_ref.dtype)


def _moba(pb, pf, km, e):
    b, s, _ = pb.shape
    tq = min(ATT_TQ, s)
    assert s % tq == 0 and tq % MOBA_BLOCK == 0 and s // MOBA_BLOCK <= LANES
    col = lambda part: (lambda bi, hp, i: (bi, i, part * BLOCKS_PER_PART + hp))
    whole = lambda part: (lambda bi, hp, i: (bi, 0, part * BLOCKS_PER_PART + hp))
    return pl.pallas_call(
        _moba_kernel,
        grid=(b, BLOCKS_PER_PART, s // tq),
        in_specs=[
            pl.BlockSpec((None, tq, LANES), col(4)),
            pl.BlockSpec((None, s, LANES), whole(5)),
            pl.BlockSpec((None, s, LANES), whole(6)),
            pl.BlockSpec((s, LANES), lambda bi, hp, i: (0, 0)),
            pl.BlockSpec((None, LANES, LANES), lambda bi, hp, i: (bi, 0, hp)),
            pl.BlockSpec((None, tq, LANES), col(7)),
        ],
        out_specs=pl.BlockSpec((None, tq, LANES), lambda bi, hp, i: (bi, i, hp)),
        out_shape=jax.ShapeDtypeStruct((b, s, BRANCH_W), jnp.bfloat16),
        scratch_shapes=[
            pltpu.VMEM((HEADS_PER_BLOCK, tq, 2 * LANES), jnp.bfloat16),
            pltpu.VMEM((HEADS_PER_BLOCK, tq, LANES), jnp.float32),
            pltpu.VMEM((HEADS_PER_BLOCK, tq, LANES), jnp.float32),
            pltpu.VMEM((HEADS_PER_BLOCK, tq, LANES), jnp.float32),
        ],
        compiler_params=pltpu.CompilerParams(
            dimension_semantics=("parallel", "parallel", "arbitrary"),
            vmem_limit_bytes=VMEM_LIMIT),
        name="moba",
    )(pb, pb, pb, e, km, pf)


def _diff_kernel(q_ref, k_ref, v_ref, lam_ref, sg_ref, g_ref, o_ref,
                 qm_sc, m_sc, l_sc, acc_sc, *, lambda_init):
    tq = q_ref.shape[0]
    i = pl.program_id(2)
    lane = _lane_iota((tq, LANES))
    q = q_ref[...]
    n_maps = LANES // DIFF_HEAD_DIM
    for c in range(n_maps):
        qm_sc[c] = jnp.where((lane // DIFF_HEAD_DIM) == c, q, jnp.zeros_like(q))

    def kv(n):
        rows = pl.ds(pl.multiple_of(n * ATT_TK, ATT_TK), ATT_TK)
        return k_ref[rows, :], v_ref[rows, :]

    kb, vb = kv(i)
    causal = _lane_iota((tq, ATT_TK)) <= _row_iota((tq, ATT_TK))
    for c in range(n_maps):
        s = jnp.where(causal, _nt_dot(qm_sc[c], kb), NEG)
        _softmax_init(s, vb, m_sc, l_sc, acc_sc, c)

    def body(n, carry):
        kb, vb = kv(n)
        for c in range(n_maps):
            _softmax_update(_nt_dot(qm_sc[c], kb), vb, m_sc, l_sc, acc_sc, c)
        return carry
    lax.fori_loop(0, i, body, 0)

    lf = lam_ref[...]
    lam = (jnp.exp(jnp.sum(lf[0:1] * lf[1:2], axis=1, keepdims=True))
           - jnp.exp(jnp.sum(lf[2:3] * lf[3:4], axis=1, keepdims=True)) + lambda_init)
    o0 = acc_sc[0] / l_sc[0] - lam * (acc_sc[1] / l_sc[1])
    o1 = acc_sc[2] / l_sc[2] - lam * (acc_sc[3] / l_sc[3])
    lo = lane < HEAD_DIM
    o = jnp.where(lo, o0, o1)
    sq = o * o
    ms0 = jnp.sum(jnp.where(lo, sq, 0.0), axis=1, keepdims=True) / HEAD_DIM
    ms1 = jnp.sum(jnp.where(lo, 0.0, sq), axis=1, keepdims=True) / HEAD_DIM
    y = (o * lax.rsqrt(jnp.where(lo, ms0, ms1) + RMS_EPS)) * sg_ref[...]
    y = y * (1.0 - lambda_init)
    o_ref[...] = (y * _silu(g_ref[...])).astype(o_ref.dtype)


def _diff(pb, pf, lam, sg, lambda_init):
    b, s, _ = pb.shape
    tq = ATT_TQ
    assert tq == ATT_TK and s % tq == 0
    col = lambda part: (lambda bi, hp, i: (bi, i, part * BLOCKS_PER_PART + hp))
    whole = lambda part: (lambda bi, hp, i: (bi, 0, part * BLOCKS_PER_PART + hp))
    n_maps = LANES // DIFF_HEAD_DIM
    return pl.pallas_call(
        functools.partial(_diff_kernel, lambda_init=lambda_init),
        grid=(b, BLOCKS_PER_PART, s // tq),
        in_specs=[
            pl.BlockSpec((None, tq, LANES), col(0)),
            pl.BlockSpec((None, s, LANES), whole(1)),
            pl.BlockSpec((None, s, LANES), whole(2)),
            pl.BlockSpec((4, LANES), lambda bi, hp, i: (0, 0)),
            pl.BlockSpec((1, LANES), lambda bi, hp, i: (0, 0)),
            pl.BlockSpec((None, tq, LANES), col(3)),
        ],
        out_specs=pl.BlockSpec((None, tq, LANES), lambda bi, hp, i: (bi, i, hp)),
        out_shape=jax.ShapeDtypeStruct((b, s, BRANCH_W), jnp.bfloat16),
        scratch_shapes=[
            pltpu.VMEM((n_maps, tq, LANES), jnp.bfloat16),
            pltpu.VMEM((n_maps, tq, LANES), jnp.float32),
            pltpu.VMEM((n_maps, tq, LANES), jnp.float32),
            pltpu.VMEM((n_maps, tq, LANES), jnp.float32),
        ],
        compiler_params=pltpu.CompilerParams(
            dimension_semantics=("parallel", "parallel", "arbitrary"),
            vmem_limit_bytes=VMEM_LIMIT),
        name="diff",
    )(pb, pb, pb, lam, sg, pf)


def _loop_by_pairs(n, body):
    def pair(u, carry):
        body(2 * u, carry)
        body(2 * u + 1, carry)
        return carry
    lax.fori_loop(0, n // 2, pair, 0)

    @pl.when(n % 2 == 1)
    def _():
        body(n - 1, 0)


def _stick_kernel(q_ref, k_ref, v_ref, tri_ref, g_ref, o_ref, qh_sc, run_sc, acc_sc):
    tq = q_ref.shape[0]
    i = pl.program_id(2)
    lane = _lane_iota((tq, LANES))
    q = q_ref[...]
    for h in range(HEADS_PER_BLOCK):
        qh_sc[h] = jnp.where((lane // HEAD_DIM) == h, q, jnp.zeros_like(q))
    tri = tri_ref[...]
    n_sub = tq // TRI_BLOCK
    sign_bit = jnp.uint32(0x80000000)

    def step(n, h, before):
        rows = pl.ds(pl.multiple_of(n * tq, tq), tq)
        z = _nt_dot(qh_sc[h], k_ref[rows, :])
        neg_abs = lax.bitcast_convert_type(
            lax.bitcast_convert_type(z, jnp.uint32) | sign_bit, jnp.float32)
        sp = jnp.maximum(z, 0.0) + jnp.log2(1.0 + jnp.exp2(neg_abs))
        if before is not None:
            sp = jnp.where(before, sp, 0.0)
        run = None if before is not None else run_sc[h]
        a_parts = [None] * n_sub
        for j in reversed(range(n_sub)):
            cols = slice(j * TRI_BLOCK, (j + 1) * TRI_BLOCK)
            sj = sp[:, cols]
            log_a = z[:, cols] - sj - jnp.dot(
                sj.astype(jnp.bfloat16), tri, preferred_element_type=jnp.float32)
            if run is not None:
                log_a = log_a - _rep(run, TRI_BLOCK // LANES)
            if before is not None:
                log_a = jnp.where(before[:, cols], log_a, NEG)
            a_parts[j] = jnp.exp2(log_a).astype(jnp.bfloat16)
            rowsum = jnp.sum(sj, axis=1, keepdims=True)
            run = jnp.broadcast_to(rowsum, run_sc.shape[1:]) if run is None else run + rowsum
        pv = jnp.dot(jnp.concatenate(a_parts, axis=1), v_ref[rows, :],
                     preferred_element_type=jnp.float32)
        run_sc[h] = run
        acc_sc[h] = pv if before is not None else acc_sc[h] + pv

    strictly_before = _lane_iota((tq, tq)) < _row_iota((tq, tq))
    for h in range(HEADS_PER_BLOCK):
        step(i, h, strictly_before)

    def body(t, carry):
        for h in range(HEADS_PER_BLOCK):
            step(i - 1 - t, h, None)
        return carry
    _loop_by_pairs(i, body)

    o = jnp.where(lane < HEAD_DIM, acc_sc[0], acc_sc[1])
    o_ref[...] = (o * _silu(g_ref[...])).astype(o_ref.dtype)


def _stick(pb, pf, tri):
    b, s, _ = pb.shape
    tq = min(STICK_T, s)
    assert s % tq == 0 and tq % TRI_BLOCK == 0
    col = lambda part: (lambda bi, hp, i: (bi, i, part * BLOCKS_PER_PART + hp))
    whole = lambda part: (lambda bi, hp, i: (bi, 0, part * BLOCKS_PER_PART + hp))
    return pl.pallas_call(
        _stick_kernel,
        grid=(b, BLOCKS_PER_PART, s // tq),
        in_specs=[
            pl.BlockSpec((None, tq, LANES), col(4)),
            pl.BlockSpec((None, s, LANES), whole(5)),
            pl.BlockSpec((None, s, LANES), whole(6)),
            pl.BlockSpec((TRI_BLOCK, TRI_BLOCK), lambda bi, hp, i: (0, 0)),
            pl.BlockSpec((None, tq, LANES), col(7)),
        ],
        out_specs=pl.BlockSpec((None, tq, LANES), lambda bi, hp, i: (bi, i, hp)),
        out_shape=jax.ShapeDtypeStruct((b, s, BRANCH_W), jnp.bfloat16),
        scratch_shapes=[
            pltpu.VMEM((HEADS_PER_BLOCK, tq, LANES), jnp.bfloat16),
            pltpu.VMEM((HEADS_PER_BLOCK, tq, LANES), jnp.float32),
            pltpu.VMEM((HEADS_PER_BLOCK, tq, LANES), jnp.float32),
        ],
        compiler_params=pltpu.CompilerParams(
            dimension_semantics=("parallel", "parallel", "arbitrary"),
            vmem_limit_bytes=VMEM_LIMIT),
        name="stick",
    )(pb, pb, pb, tri, pf)


def _rope_tables(s, dim):
    half = dim // 2
    inv = ROPE_THETA ** (-jnp.arange(half, dtype=jnp.float32) / half)
    ang = jnp.arange(s).astype(jnp.float32)[:, None] * inv[None, :]
    cos = jnp.tile(jnp.cos(ang), (1, LANES // half))
    sin = jnp.tile(jnp.sin(ang), (1, LANES // half))
    sign = jnp.where((jnp.arange(LANES) % dim) < half, -1.0, 1.0).astype(jnp.float32)
    return cos, sin * sign[None, :]


def kernel(x, norm_g, w_in, w_out, diff_lam, diff_subln_g, final_norm_g):
    b, s, d = x.shape
    depth = w_in.shape[0]
    w_in_b = w_in.astype(jnp.bfloat16)
    w_out_b = w_out.astype(jnp.bfloat16)
    norm_g3 = norm_g.reshape(depth, 1, d)
    fg = final_norm_g.reshape(1, d)
    rope_even = _rope_tables(s, HEAD_DIM)
    rope_odd = _rope_tables(s, DIFF_HEAD_DIM)
    n_moba = s // MOBA_BLOCK
    block_code = (jnp.arange(s)[:, None] // MOBA_BLOCK == jnp.arange(LANES)[None, :]
                  ).astype(jnp.bfloat16)
    tri = (jnp.arange(TRI_BLOCK)[:, None] > jnp.arange(TRI_BLOCK)[None, :]).astype(jnp.bfloat16)

    for layer in range(depth):
        even = layer % 2 == 0
        if even:
            pf, pb, km = _inproj(x, norm_g3, w_in_b, layer, *rope_even, even=True)
            km = km.reshape(b, n_moba, BRANCH_W)
            km = jnp.pad(km, ((0, 0), (0, LANES - n_moba), (0, 0)))
            y1 = _dilated(pf)
            y2 = _moba(pb, pf, km, block_code)
        else:
            pf, pb = _inproj(x, norm_g3, w_in_b, layer, *rope_odd, even=False)
            li = layer // 2
            lambda_init = 0.8 - 0.6 * math.exp(-0.3 * layer)
            lam = jnp.pad(diff_lam[li], ((0, 0), (0, LANES - DIFF_HEAD_DIM)))
            sg = jnp.tile(diff_subln_g[li], HEADS_PER_BLOCK).reshape(1, LANES)
            y1 = _diff(pb, pf, lam, sg, lambda_init)
            y2 = _stick(pb, pf, tri)
        x = _outproj(x, y1, y2, w_out_b, layer, fg, final=(layer == depth - 1))
    return x
```

```python
import functools
import math

import jax
import jax.numpy as jnp
from jax import lax
from jax.experimental import pallas as pl
from jax.experimental.pallas import tpu as pltpu

HEAD_DIM = 64
N_HEADS = 4
BRANCH_W = N_HEADS * HEAD_DIM
N_IN_PARTS = 8
PROJ_W = N_IN_PARTS * BRANCH_W
LANES = 128
HEADS_PER_BLOCK = LANES // HEAD_DIM
BLOCKS_PER_PART = BRANCH_W // LANES
ROPE_THETA = 10000.0
DIL_RATES = (1, 4, 16)
DIL_STEPS = 128
DIL_SUPER = DIL_STEPS * max(DIL_RATES)
MOBA_BLOCK = 256
MOBA_TOPK = 3
DIFF_HEAD_DIM = HEAD_DIM // 2
RMS_EPS = 1e-6
NEG = -1e30
LOG2E = math.log2(math.e)

ATT_TQ = 1024
ATT_TK = 1024
STICK_T = 512
STICK_DEAD_BITS = 160.0
TRI_BLOCK = 256
PROJ_TM = 512
VMEM_LIMIT = 56 * 1024 * 1024

_NT = (((1,), (1,)), ((), ()))


def _nt_dot(a, b, precision=None):
    return lax.dot_general(a, b, _NT, preferred_element_type=jnp.float32, precision=precision)


def _lane_iota(shape):
    return lax.broadcasted_iota(jnp.int32, shape, 1)


def _row_iota(shape):
    return lax.broadcasted_iota(jnp.int32, shape, 0)


def _silu(g):
    return g * (1.0 / (1.0 + jnp.exp(-g)))


def _rep(a, n):
    return a if n == 1 else jnp.concatenate([a] * n, axis=1)


def _inproj_kernel(x_ref, g_ref, w_ref, cos_ref, sin_ref, pf_ref, pb_ref, *rest, even):
    x = x_ref[...]
    ms = jnp.mean(x * x, axis=-1, keepdims=True)
    h = (x * lax.rsqrt(ms + RMS_EPS)) * g_ref[...]
    proj = jnp.dot(h.astype(jnp.bfloat16), w_ref[...], preferred_element_type=jnp.float32)

    if even:
        rope_parts, shift = (0, 1, 4, 5), HEAD_DIM // 2
        scales = {0: LOG2E / math.sqrt(HEAD_DIM), 4: LOG2E / math.sqrt(HEAD_DIM)}
    else:
        rope_parts, shift = (0, 1), DIFF_HEAD_DIM // 2
        scales = {0: LOG2E / math.sqrt(DIFF_HEAD_DIM), 4: LOG2E / math.sqrt(HEAD_DIM)}
    cos = cos_ref[...]
    sin = sin_ref[...]
    first_half = (_lane_iota(cos.shape) % (2 * shift)) < shift

    for c in range(PROJ_W // LANES):
        part = c // BLOCKS_PER_PART
        t = proj[:, c * LANES:(c + 1) * LANES]
        if part in rope_parts:
            up = pltpu.roll(t, LANES - shift, axis=1)
            down = pltpu.roll(t, shift, axis=1)
            t = t * cos + jnp.where(first_half, up, down) * sin
        if part in scales:
            t = t * scales[part]
        pf_ref[:, c * LANES:(c + 1) * LANES] = t
        pb_ref[:, c * LANES:(c + 1) * LANES] = t.astype(jnp.bfloat16)
        if even and part == 5:
            km_ref = rest[0]
            cb = c - 5 * BLOCKS_PER_PART
            nblk = t.shape[0] // MOBA_BLOCK
            km_ref[:, cb * LANES:(cb + 1) * LANES] = jnp.mean(
                t.reshape(nblk, MOBA_BLOCK, LANES), axis=1)


def _inproj(x, g, w_b, layer, cos, sin, even):
    b, s, d = x.shape
    tm = min(PROJ_TM, s)
    assert s % tm == 0 and tm % MOBA_BLOCK == 0
    ni = s // tm
    out_shape = [jax.ShapeDtypeStruct((b, s, PROJ_W), jnp.float32),
                 jax.ShapeDtypeStruct((b, s, PROJ_W), jnp.bfloat16)]
    out_specs = [pl.BlockSpec((None, tm, PROJ_W), lambda bi, i: (bi, i, 0)),
                 pl.BlockSpec((None, tm, PROJ_W), lambda bi, i: (bi, i, 0))]
    if even:
        out_shape.append(jax.ShapeDtypeStruct((b, ni, tm // MOBA_BLOCK, BRANCH_W), jnp.float32))
        out_specs.append(pl.BlockSpec((None, None, tm // MOBA_BLOCK, BRANCH_W),
                                      lambda bi, i: (bi, i, 0, 0)))
    return pl.pallas_call(
        functools.partial(_inproj_kernel, even=even),
        grid=(b, ni),
        in_specs=[
            pl.BlockSpec((None, tm, d), lambda bi, i: (bi, i, 0)),
            pl.BlockSpec((None, 1, d), lambda bi, i: (layer, 0, 0)),
            pl.BlockSpec((None, d, PROJ_W), lambda bi, i: (layer, 0, 0)),
            pl.BlockSpec((tm, LANES), lambda bi, i: (i, 0)),
            pl.BlockSpec((tm, LANES), lambda bi, i: (i, 0)),
        ],
        out_specs=out_specs,
        out_shape=out_shape,
        compiler_params=pltpu.CompilerParams(
            dimension_semantics=("parallel", "parallel"), vmem_limit_bytes=VMEM_LIMIT),
        name="inproj_even" if even else "inproj_odd",
    )(x, g, w_b, cos, sin)


def _outproj_kernel(x_ref, y1_ref, y2_ref, w_ref, fg_ref, o_ref, *, final):
    y = jnp.concatenate([y1_ref[...], y2_ref[...]], axis=1)
    xn = x_ref[...] + jnp.dot(y, w_ref[...], preferred_element_type=jnp.float32)
    if final:
        ms = jnp.mean(xn * xn, axis=-1, keepdims=True)
        xn = (xn * lax.rsqrt(ms + RMS_EPS)) * fg_ref[...]
    o_ref[...] = xn


def _outproj(x, y1, y2, w_b, layer, fg, final):
    b, s, d = x.shape
    tm = min(PROJ_TM, s)
    return pl.pallas_call(
        functools.partial(_outproj_kernel, final=final),
        grid=(b, s // tm),
        in_specs=[
            pl.BlockSpec((None, tm, d), lambda bi, i: (bi, i, 0)),
            pl.BlockSpec((None, tm, BRANCH_W), lambda bi, i: (bi, i, 0)),
            pl.BlockSpec((None, tm, BRANCH_W), lambda bi, i: (bi, i, 0)),
            pl.BlockSpec((None, 2 * BRANCH_W, d), lambda bi, i: (layer, 0, 0)),
            pl.BlockSpec((1, d), lambda bi, i: (0, 0)),
        ],
        out_specs=pl.BlockSpec((None, tm, d), lambda bi, i: (bi, i, 0)),
        out_shape=jax.ShapeDtypeStruct((b, s, d), jnp.float32),
        compiler_params=pltpu.CompilerParams(
            dimension_semantics=("parallel", "parallel"), vmem_limit_bytes=VMEM_LIMIT),
        name="outproj_final" if final else "outproj",
    )(x, y1, y2, w_b, fg)


def _dilated_kernel(q_ref, kc_ref, kp_ref, vc_ref, vp_ref, g_ref, o_ref,
                    kwin, vwin, m_sc, l_sc, acc_sc):
    sb = q_ref.shape[0]
    first_super = pl.program_id(2) == 0
    kwin[0:sb, :] = kp_ref[...]
    kwin[sb:2 * sb, :] = kc_ref[...]
    vwin[0:sb, :] = vp_ref[...]
    vwin[sb:2 * sb, :] = vc_ref[...]

    t = DIL_STEPS
    lane = _lane_iota((t, LANES))
    row = _row_iota((t, 2 * t))
    col = _lane_iota((t, 2 * t))
    band = (col >= row) & (col <= row + t)

    for ri, r in enumerate(DIL_RATES):
        n_tiles = sb // t

        def tile(j, carry, r=r, ri=ri):
            phase = j % r
            nb = j // r
            q0 = phase + r * t * nb
            rows = pl.ds(q0, t, stride=r) if r > 1 else pl.ds(q0, t)
            own = pl.ds(sb + q0, t, stride=r) if r > 1 else pl.ds(sb + q0, t)
            prev = pl.ds(sb + q0 - r * t, t, stride=r) if r > 1 else pl.ds(sb + q0 - r * t, t)
            qt = q_ref[rows, :]
            kk = jnp.concatenate([kwin[prev, :], kwin[own, :]], axis=0).astype(jnp.bfloat16)
            vv = jnp.concatenate([vwin[prev, :], vwin[own, :]], axis=0).astype(jnp.bfloat16)
            no_prev = jnp.logical_and(first_super, nb == 0)
            mask = band & (col >= jnp.where(no_prev, t, 0))
            ms, ls, accs = [], [], []
            for h in range(HEADS_PER_BLOCK):
                hm = (lane // HEAD_DIM) == h
                qh = jnp.where(hm, qt, 0.0).astype(jnp.bfloat16)
                s = jnp.where(mask, _nt_dot(qh, kk), NEG)
                m = jnp.max(s, axis=1, keepdims=True)
                p = jnp.exp2(s - m)
                ms.append(m)
                ls.append(jnp.sum(p, axis=1, keepdims=True))
                accs.append(jnp.dot(p.astype(jnp.bfloat16), vv, preferred_element_type=jnp.float32))
            lo = lane < HEAD_DIM
            m_sc[ri, rows, :] = jnp.where(lo, ms[0], ms[1])
            l_sc[ri, rows, :] = jnp.where(lo, ls[0], ls[1])
            acc_sc[ri, rows, :] = jnp.where(lo, accs[0], accs[1])
            return carry

        lax.fori_loop(0, n_tiles, tile, 0, unroll=8)

    chunk = 256
    def combine(c, carry):
        rows = pl.ds(pl.multiple_of(c * chunk, chunk), chunk)
        m0, m1, m2 = m_sc[0, rows, :], m_sc[1, rows, :], m_sc[2, rows, :]
        mx = jnp.maximum(jnp.maximum(m0, m1), m2)
        w0, w1, w2 = jnp.exp2(m0 - mx), jnp.exp2(m1 - mx), jnp.exp2(m2 - mx)
        num = w0 * acc_sc[0, rows, :] + w1 * acc_sc[1, rows, :] + w2 * acc_sc[2, rows, :]
        den = w0 * l_sc[0, rows, :] + w1 * l_sc[1, rows, :] + w2 * l_sc[2, rows, :]
        o_ref[rows, :] = ((num / den) * _silu(g_ref[rows, :])).astype(o_ref.dtype)
        return carry
    lax.fori_loop(0, sb // chunk, combine, 0)


def _dilated(pf):
    b, s, _ = pf.shape
    sb = DIL_SUPER
    assert s % sb == 0
    blk = lambda part, prev: pl.BlockSpec(
        (None, sb, LANES),
        (lambda bi, hp, i: (bi, jnp.maximum(i - 1, 0), part * BLOCKS_PER_PART + hp)) if prev
        else (lambda bi, hp, i: (bi, i, part * BLOCKS_PER_PART + hp)))
    return pl.pallas_call(
        _dilated_kernel,
        grid=(b, BLOCKS_PER_PART, s // sb),
        in_specs=[blk(0, False), blk(1, False), blk(1, True), blk(2, False), blk(2, True),
                  blk(3, False)],
        out_specs=pl.BlockSpec((None, sb, LANES), lambda bi, hp, i: (bi, i, hp)),
        out_shape=jax.ShapeDtypeStruct((b, s, BRANCH_W), jnp.bfloat16),
        scratch_shapes=[
            pltpu.VMEM((2 * sb, LANES), jnp.float32),
            pltpu.VMEM((2 * sb, LANES), jnp.float32),
            pltpu.VMEM((len(DIL_RATES), sb, LANES), jnp.float32),
            pltpu.VMEM((len(DIL_RATES), sb, LANES), jnp.float32),
            pltpu.VMEM((len(DIL_RATES), sb, LANES), jnp.float32),
        ],
        compiler_params=pltpu.CompilerParams(
            dimension_semantics=("parallel", "parallel", "arbitrary"),
            vmem_limit_bytes=VMEM_LIMIT),
        name="dilated",
    )(pf, pf, pf, pf, pf, pf)


def _softmax_init(s, v, m_sc, l_sc, acc_sc, idx):
    m = jnp.max(s, axis=1, keepdims=True)
    p = jnp.exp2(s - m)
    m_sc[idx] = jnp.broadcast_to(m, m_sc.shape[1:])
    l_sc[idx] = jnp.broadcast_to(jnp.sum(p, axis=1, keepdims=True), l_sc.shape[1:])
    acc_sc[idx] = jnp.dot(p.astype(jnp.bfloat16), v, preferred_element_type=jnp.float32)


def _softmax_update(s, v, m_sc, l_sc, acc_sc, idx):
    m_prev = m_sc[idx]
    m_new = jnp.maximum(m_prev, jnp.max(s, axis=1, keepdims=True))
    alpha = jnp.exp2(m_prev - m_new)
    p = jnp.exp2(s - _rep(m_new, s.shape[1] // LANES))
    m_sc[idx] = m_new
    l_sc[idx] = alpha * l_sc[idx] + jnp.sum(p, axis=1, keepdims=True)
    acc_sc[idx] = alpha * acc_sc[idx] + jnp.dot(
        p.astype(jnp.bfloat16), v, preferred_element_type=jnp.float32)


def _moba_kernel(q_ref, k_ref, v_ref, e_ref, km_ref, g_ref, o_ref,
                 qa_sc, m_sc, l_sc, acc_sc):
    tq = q_ref.shape[0]
    i = pl.program_id(2)
    lane = _lane_iota((tq, LANES))
    cur = i * (tq // MOBA_BLOCK) + _row_iota((tq, LANES)) // MOBA_BLOCK
    q = q_ref[...]
    km = km_ref[...]
    for h in range(HEADS_PER_BLOCK):
        hm = (lane // HEAD_DIM) == h
        qh = jnp.where(hm, q, jnp.zeros_like(q))
        gate = _nt_dot(qh.astype(jnp.float32), km, precision=lax.Precision.HIGHEST)
        avail = lane < cur
        gate = jnp.where(avail, gate, -jnp.inf)
        sel = lane == cur
        for _ in range(MOBA_TOPK):
            mx = jnp.max(gate, axis=1, keepdims=True)
            first = jnp.min(jnp.where(gate == mx, lane, LANES), axis=1, keepdims=True)
            pick = (lane == first) & (mx > -jnp.inf)
            sel = sel | pick
            gate = jnp.where(pick, -jnp.inf, gate)
        bias = jnp.where(sel, 0.0, NEG).astype(jnp.bfloat16)
        qa_sc[h] = jnp.concatenate([qh, bias], axis=1)

    def kv(n):
        rows = pl.ds(pl.multiple_of(n * tq, tq), tq)
        return jnp.concatenate([k_ref[rows, :], e_ref[rows, :]], axis=1), v_ref[rows, :]

    ka, vb = kv(i)
    causal = _lane_iota((tq, tq)) <= _row_iota((tq, tq))
    for h in range(HEADS_PER_BLOCK):
        s = jnp.where(causal, _nt_dot(qa_sc[h], ka), NEG)
        _softmax_init(s, vb, m_sc, l_sc, acc_sc, h)

    def body(n, carry):
        ka, vb = kv(n)
        for h in range(HEADS_PER_BLOCK):
            _softmax_update(_nt_dot(qa_sc[h], ka), vb, m_sc, l_sc, acc_sc, h)
        return carry
    lax.fori_loop(0, i, body, 0)

    o = jnp.where(lane < HEAD_DIM, acc_sc[0] / l_sc[0], acc_sc[1] / l_sc[1])
    o_ref[...] = (o * _silu(g_ref[...])).astype(o_ref.dtype)


def _moba(pb, pf, km, e):
    b, s, _ = pb.shape
    tq = min(ATT_TQ, s)
    assert s % tq == 0 and tq % MOBA_BLOCK == 0 and s // MOBA_BLOCK <= LANES
    col = lambda part: (lambda bi, hp, i: (bi, i, part * BLOCKS_PER_PART + hp))
    whole = lambda part: (lambda bi, hp, i: (bi, 0, part * BLOCKS_PER_PART + hp))
    return pl.pallas_call(
        _moba_kernel,
        grid=(b, BLOCKS_PER_PART, s // tq),
        in_specs=[
            pl.BlockSpec((None, tq, LANES), col(4)),
            pl.BlockSpec((None, s, LANES), whole(5)),
            pl.BlockSpec((None, s, LANES), whole(6)),
            pl.BlockSpec((s, LANES), lambda bi, hp, i: (0, 0)),
            pl.BlockSpec((None, LANES, LANES), lambda bi, hp, i: (bi, 0, hp)),
            pl.BlockSpec((None, tq, LANES), col(7)),
        ],
        out_specs=pl.BlockSpec((None, tq, LANES), lambda bi, hp, i: (bi, i, hp)),
        out_shape=jax.ShapeDtypeStruct((b, s, BRANCH_W), jnp.bfloat16),
        scratch_shapes=[
            pltpu.VMEM((HEADS_PER_BLOCK, tq, 2 * LANES), jnp.bfloat16),
            pltpu.VMEM((HEADS_PER_BLOCK, tq, LANES), jnp.float32),
            pltpu.VMEM((HEADS_PER_BLOCK, tq, LANES), jnp.float32),
            pltpu.VMEM((HEADS_PER_BLOCK, tq, LANES), jnp.float32),
        ],
        compiler_params=pltpu.CompilerParams(
            dimension_semantics=("parallel", "parallel", "arbitrary"),
            vmem_limit_bytes=VMEM_LIMIT),
        name="moba",
    )(pb, pb, pb, e, km, pf)


def _diff_kernel(q_ref, k_ref, v_ref, lam_ref, sg_ref, g_ref, o_ref,
                 qm_sc, m_sc, l_sc, acc_sc, *, lambda_init):
    tq = q_ref.shape[0]
    i = pl.program_id(2)
    lane = _lane_iota((tq, LANES))
    q = q_ref[...]
    n_maps = LANES // DIFF_HEAD_DIM
    for c in range(n_maps):
        qm_sc[c] = jnp.where((lane // DIFF_HEAD_DIM) == c, q, jnp.zeros_like(q))

    def kv(n):
        rows = pl.ds(pl.multiple_of(n * ATT_TK, ATT_TK), ATT_TK)
        return k_ref[rows, :], v_ref[rows, :]

    kb, vb = kv(i)
    causal = _lane_iota((tq, ATT_TK)) <= _row_iota((tq, ATT_TK))
    for c in range(n_maps):
        s = jnp.where(causal, _nt_dot(qm_sc[c], kb), NEG)
        _softmax_init(s, vb, m_sc, l_sc, acc_sc, c)

    def body(n, carry):
        kb, vb = kv(n)
        for c in range(n_maps):
            _softmax_update(_nt_dot(qm_sc[c], kb), vb, m_sc, l_sc, acc_sc, c)
        return carry
    lax.fori_loop(0, i, body, 0)

    lf = lam_ref[...]
    lam = (jnp.exp(jnp.sum(lf[0:1] * lf[1:2], axis=1, keepdims=True))
           - jnp.exp(jnp.sum(lf[2:3] * lf[3:4], axis=1, keepdims=True)) + lambda_init)
    o0 = acc_sc[0] / l_sc[0] - lam * (acc_sc[1] / l_sc[1])
    o1 = acc_sc[2] / l_sc[2] - lam * (acc_sc[3] / l_sc[3])
    lo = lane < HEAD_DIM
    o = jnp.where(lo, o0, o1)
    sq = o * o
    ms0 = jnp.sum(jnp.where(lo, sq, 0.0), axis=1, keepdims=True) / HEAD_DIM
    ms1 = jnp.sum(jnp.where(lo, 0.0, sq), axis=1, keepdims=True) / HEAD_DIM
    y = (o * lax.rsqrt(jnp.where(lo, ms0, ms1) + RMS_EPS)) * sg_ref[...]
    y = y * (1.0 - lambda_init)
    o_ref[...] = (y * _silu(g_ref[...])).astype(o_ref.dtype)


def _diff(pb, pf, lam, sg, lambda_init):
    b, s, _ = pb.shape
    tq = ATT_TQ
    assert tq == ATT_TK and s % tq == 0
    col = lambda part: (lambda bi, hp, i: (bi, i, part * BLOCKS_PER_PART + hp))
    whole = lambda part: (lambda bi, hp, i: (bi, 0, part * BLOCKS_PER_PART + hp))
    n_maps = LANES // DIFF_HEAD_DIM
    return pl.pallas_call(
        functools.partial(_diff_kernel, lambda_init=lambda_init),
        grid=(b, BLOCKS_PER_PART, s // tq),
        in_specs=[
            pl.BlockSpec((None, tq, LANES), col(0)),
            pl.BlockSpec((None, s, LANES), whole(1)),
            pl.BlockSpec((None, s, LANES), whole(2)),
            pl.BlockSpec((4, LANES), lambda bi, hp, i: (0, 0)),
            pl.BlockSpec((1, LANES), lambda bi, hp, i: (0, 0)),
            pl.BlockSpec((None, tq, LANES), col(3)),
        ],
        out_specs=pl.BlockSpec((None, tq, LANES), lambda bi, hp, i: (bi, i, hp)),
        out_shape=jax.ShapeDtypeStruct((b, s, BRANCH_W), jnp.bfloat16),
        scratch_shapes=[
            pltpu.VMEM((n_maps, tq, LANES), jnp.bfloat16),
            pltpu.VMEM((n_maps, tq, LANES), jnp.float32),
            pltpu.VMEM((n_maps, tq, LANES), jnp.float32),
            pltpu.VMEM((n_maps, tq, LANES), jnp.float32),
        ],
        compiler_params=pltpu.CompilerParams(
            dimension_semantics=("parallel", "parallel", "arbitrary"),
            vmem_limit_bytes=VMEM_LIMIT),
        name="diff",
    )(pb, pb, pb, lam, sg, pf)


def _stick_kernel(q_ref, k_ref, v_ref, tri_ref, g_ref, o_ref, qh_sc, run_sc, acc_sc):
    tq = q_ref.shape[0]
    i = pl.program_id(2)
    lane = _lane_iota((tq, LANES))
    q = q_ref[...]
    for h in range(HEADS_PER_BLOCK):
        qh_sc[h] = jnp.where((lane // HEAD_DIM) == h, q, jnp.zeros_like(q))
    tri = tri_ref[...]
    n_sub = tq // TRI_BLOCK
    sign_bit = jnp.uint32(0x80000000)

    def step(n, h, before):
        rows = pl.ds(pl.multiple_of(n * tq, tq), tq)
        z = _nt_dot(qh_sc[h], k_ref[rows, :])
        neg_abs = lax.bitcast_convert_type(
            lax.bitcast_convert_type(z, jnp.uint32) | sign_bit, jnp.float32)
        sp = jnp.maximum(z, 0.0) + jnp.log2(1.0 + jnp.exp2(neg_abs))
        if before is not None:
            sp = jnp.where(before, sp, 0.0)
        run = None if before is not None else run_sc[h]
        a_parts = [None] * n_sub
        for j in reversed(range(n_sub)):
            cols = slice(j * TRI_BLOCK, (j + 1) * TRI_BLOCK)
            sj = sp[:, cols]
            log_a = z[:, cols] - sj - jnp.dot(
                sj.astype(jnp.bfloat16), tri, preferred_element_type=jnp.float32)
            if run is not None:
                log_a = log_a - _rep(run, TRI_BLOCK // LANES)
            if before is not None:
                log_a = jnp.where(before[:, cols], log_a, NEG)
            a_parts[j] = jnp.exp2(log_a).astype(jnp.bfloat16)
            rowsum = jnp.sum(sj, axis=1, keepdims=True)
            run = jnp.broadcast_to(rowsum, run_sc.shape[1:]) if run is None else run + rowsum
        pv = jnp.dot(jnp.concatenate(a_parts, axis=1), v_ref[rows, :],
                     preferred_element_type=jnp.float32)
        run_sc[h] = run
        acc_sc[h] = pv if before is not None else acc_sc[h] + pv

    strictly_before = _lane_iota((tq, tq)) < _row_iota((tq, tq))
    for h in range(HEADS_PER_BLOCK):
        step(i, h, strictly_before)

    def alive():
        return jnp.min(jnp.minimum(run_sc[0], run_sc[1])) < STICK_DEAD_BITS

    def cond(carry):
        t, live = carry
        return jnp.logical_and(t < i, live)

    def body(carry):
        t, _ = carry
        for h in range(HEADS_PER_BLOCK):
            step(i - 1 - t, h, None)
        return t + 1, alive()

    lax.while_loop(cond, body, (jnp.int32(0), alive()))

    o = jnp.where(lane < HEAD_DIM, acc_sc[0], acc_sc[1])
    o_ref[...] = (o * _silu(g_ref[...])).astype(o_ref.dtype)


def _stick(pb, pf, tri):
    b, s, _ = pb.shape
    tq = min(STICK_T, s)
    assert s % tq == 0 and tq % TRI_BLOCK == 0
    col = lambda part: (lambda bi, hp, i: (bi, i, part * BLOCKS_PER_PART + hp))
    whole = lambda part: (lambda bi, hp, i: (bi, 0, part * BLOCKS_PER_PART + hp))
    return pl.pallas_call(
        _stick_kernel,
        grid=(b, BLOCKS_PER_PART, s // tq),
        in_specs=[
            pl.BlockSpec((None, tq, LANES), col(4)),
            pl.BlockSpec((None, s, LANES), whole(5)),
            pl.BlockSpec((None, s, LANES), whole(6)),
            pl.BlockSpec((TRI_BLOCK, TRI_BLOCK), lambda bi, hp, i: (0, 0)),
            pl.BlockSpec((None, tq, LANES), col(7)),
        ],
        out_specs=pl.BlockSpec((None, tq, LANES), lambda bi, hp, i: (bi, i, hp)),
        out_shape=jax.ShapeDtypeStruct((b, s, BRANCH_W), jnp.bfloat16),
        scratch_shapes=[
            pltpu.VMEM((HEADS_PER_BLOCK, tq, LANES), jnp.bfloat16),
            pltpu.VMEM((HEADS_PER_BLOCK, tq, LANES), jnp.float32),
            pltpu.VMEM((HEADS_PER_BLOCK, tq, LANES), jnp.float32),
        ],
        compiler_params=pltpu.CompilerParams(
            dimension_semantics=("parallel", "parallel", "arbitrary"),
            vmem_limit_bytes=VMEM_LIMIT),
        name="stick",
    )(pb, pb, pb, tri, pf)


def _rope_tables(s, dim):
    half = dim // 2
    inv = ROPE_THETA ** (-jnp.arange(half, dtype=jnp.float32) / half)
    ang = jnp.arange(s).astype(jnp.float32)[:, None] * inv[None, :]
    cos = jnp.tile(jnp.cos(ang), (1, LANES // half))
    sin = jnp.tile(jnp.sin(ang), (1, LANES // half))
    sign = jnp.where((jnp.arange(LANES) % dim) < half, -1.0, 1.0).astype(jnp.float32)
    return cos, sin * sign[None, :]


def kernel(x, norm_g, w_in, w_out, diff_lam, diff_subln_g, final_norm_g):
    b, s, d = x.shape
    depth = w_in.shape[0]
    w_in_b = w_in.astype(jnp.bfloat16)
    w_out_b = w_out.astype(jnp.bfloat16)
    norm_g3 = norm_g.reshape(depth, 1, d)
    fg = final_norm_g.reshape(1, d)
    rope_even = _rope_tables(s, HEAD_DIM)
    rope_odd = _rope_tables(s, DIFF_HEAD_DIM)
    n_moba = s // MOBA_BLOCK
    block_code = (jnp.arange(s)[:, None] // MOBA_BLOCK == jnp.arange(LANES)[None, :]
                  ).astype(jnp.bfloat16)
    tri = (jnp.arange(TRI_BLOCK)[:, None] > jnp.arange(TRI_BLOCK)[None, :]).astype(jnp.bfloat16)

    for layer in range(depth):
        even = layer % 2 == 0
        if even:
            pf, pb, km = _inproj(x, norm_g3, w_in_b, layer, *rope_even, even=True)
            km = km.reshape(b, n_moba, BRANCH_W)
            km = jnp.pad(km, ((0, 0), (0, LANES - n_moba), (0, 0)))
            y1 = _dilated(pf)
            y2 = _moba(pb, pf, km, block_code)
        else:
            pf, pb = _inproj(x, norm_g3, w_in_b, layer, *rope_odd, even=False)
            li = layer // 2
            lambda_init = 0.8 - 0.6 * math.exp(-0.3 * layer)
            lam = jnp.pad(diff_lam[li], ((0, 0), (0, LANES - DIFF_HEAD_DIM)))
            sg = jnp.tile(diff_subln_g[li], HEADS_PER_BLOCK).reshape(1, LANES)
            y1 = _diff(pb, pf, lam, sg, lambda_init)
            y2 = _stick(pb, pf, tri)
        x = _outproj(x, y1, y2, w_out_b, layer, fg, final=(layer == depth - 1))
    return x
```

```python
import functools
import math

import jax
import jax.numpy as jnp
from jax import lax
from jax.experimental import pallas as pl
from jax.experimental.pallas import tpu as pltpu

HEAD_DIM = 64
N_HEADS = 4
BRANCH_W = N_HEADS * HEAD_DIM
N_IN_PARTS = 8
PROJ_W = N_IN_PARTS * BRANCH_W
LANES = 128
HEADS_PER_BLOCK = LANES // HEAD_DIM
BLOCKS_PER_PART = BRANCH_W // LANES
ROPE_THETA = 10000.0
DIL_RATES = (1, 4, 16)
DIL_STEPS = 128
DIL_SUPER = DIL_STEPS * max(DIL_RATES)
MOBA_BLOCK = 256
MOBA_TOPK = 3
DIFF_HEAD_DIM = HEAD_DIM // 2
RMS_EPS = 1e-6
NEG = -1e30
LOG2E = math.log2(math.e)

ATT_TQ = 1024
ATT_TK = 1024
STICK_T = 512
STICK_DEAD_BITS = 160.0
TRI_BLOCK = 256
PROJ_TM = 512
F32_PARTS_EVEN = (0, 1, 2, 3, 7)
F32_PARTS_ODD = (3, 7)
VMEM_LIMIT = 56 * 1024 * 1024

_NT = (((1,), (1,)), ((), ()))


def _nt_dot(a, b, precision=None):
    return lax.dot_general(a, b, _NT, preferred_element_type=jnp.float32, precision=precision)


def _lane_iota(shape):
    return lax.broadcasted_iota(jnp.int32, shape, 1)


def _row_iota(shape):
    return lax.broadcasted_iota(jnp.int32, shape, 0)


def _silu(g):
    return g * (1.0 / (1.0 + jnp.exp(-g)))


def _rep(a, n):
    return a if n == 1 else jnp.concatenate([a] * n, axis=1)


def _proj_kernel(*refs, even, has_prev):
    if has_prev:
        (x_ref, y1_ref, y2_ref, wo_ref, g_ref, w_ref, cos_ref, sin_ref,
         xn_ref, pf_ref, pb_ref, *rest) = refs
        y = jnp.concatenate([y1_ref[...], y2_ref[...]], axis=1)
        x = x_ref[...] + jnp.dot(y, wo_ref[...], preferred_element_type=jnp.float32)
        xn_ref[...] = x
    else:
        x_ref, g_ref, w_ref, cos_ref, sin_ref, pf_ref, pb_ref, *rest = refs
        x = x_ref[...]
    ms = jnp.mean(x * x, axis=-1, keepdims=True)
    h = (x * lax.rsqrt(ms + RMS_EPS)) * g_ref[...]
    proj = jnp.dot(h.astype(jnp.bfloat16), w_ref[...], preferred_element_type=jnp.float32)

    if even:
        rope_parts, shift, f32_parts = (0, 1, 4, 5), HEAD_DIM // 2, F32_PARTS_EVEN
        scales = {0: LOG2E / math.sqrt(HEAD_DIM), 4: LOG2E / math.sqrt(HEAD_DIM)}
    else:
        rope_parts, shift, f32_parts = (0, 1), DIFF_HEAD_DIM // 2, F32_PARTS_ODD
        scales = {0: LOG2E / math.sqrt(DIFF_HEAD_DIM), 4: LOG2E / math.sqrt(HEAD_DIM)}
    cos = cos_ref[...]
    sin = sin_ref[...]
    first_half = (_lane_iota(cos.shape) % (2 * shift)) < shift

    for c in range(PROJ_W // LANES):
        part = c // BLOCKS_PER_PART
        t = proj[:, c * LANES:(c + 1) * LANES]
        if part in rope_parts:
            up = pltpu.roll(t, LANES - shift, axis=1)
            down = pltpu.roll(t, shift, axis=1)
            t = t * cos + jnp.where(first_half, up, down) * sin
        if part in scales:
            t = t * scales[part]
        if part in f32_parts:
            cf = f32_parts.index(part) * BLOCKS_PER_PART + c % BLOCKS_PER_PART
            pf_ref[:, cf * LANES:(cf + 1) * LANES] = t
        pb_ref[:, c * LANES:(c + 1) * LANES] = t.astype(jnp.bfloat16)
        if even and part == 5:
            km_ref = rest[0]
            cb = c - 5 * BLOCKS_PER_PART
            nblk = t.shape[0] // MOBA_BLOCK
            km_ref[:, cb * LANES:(cb + 1) * LANES] = jnp.mean(
                t.reshape(nblk, MOBA_BLOCK, LANES), axis=1)


def _proj(x, g, w_b, layer, cos, sin, even, prev=None):
    b, s, d = x.shape
    tm = min(PROJ_TM, s)
    assert s % tm == 0 and tm % MOBA_BLOCK == 0
    ni = s // tm
    f32_w = len(F32_PARTS_EVEN if even else F32_PARTS_ODD) * BRANCH_W
    row = lambda w: pl.BlockSpec((None, tm, w), lambda bi, i: (bi, i, 0))
    const = lambda shape, idx: pl.BlockSpec(shape, lambda bi, i: idx, pipeline_mode=pl.Buffered(1))
    out_shape = [jax.ShapeDtypeStruct((b, s, f32_w), jnp.float32),
                 jax.ShapeDtypeStruct((b, s, PROJ_W), jnp.bfloat16)]
    out_specs = [row(f32_w), row(PROJ_W)]
    if even:
        out_shape.append(jax.ShapeDtypeStruct((b, ni, tm // MOBA_BLOCK, BRANCH_W), jnp.float32))
        out_specs.append(pl.BlockSpec((None, None, tm // MOBA_BLOCK, BRANCH_W),
                                      lambda bi, i: (bi, i, 0, 0)))
    in_specs = [row(d)]
    args = [x]
    if prev is not None:
        y1, y2, w_out_b = prev
        in_specs += [row(BRANCH_W), row(BRANCH_W),
                     const((None, 2 * BRANCH_W, d), (layer - 1, 0, 0))]
        args += [y1, y2, w_out_b]
        out_shape.insert(0, jax.ShapeDtypeStruct((b, s, d), jnp.float32))
        out_specs.insert(0, row(d))
    in_specs += [const((None, 1, d), (layer, 0, 0)), const((None, d, PROJ_W), (layer, 0, 0)),
                 pl.BlockSpec((tm, LANES), lambda bi, i: (i, 0)),
                 pl.BlockSpec((tm, LANES), lambda bi, i: (i, 0))]
    args += [g, w_b, cos, sin]
    return pl.pallas_call(
        functools.partial(_proj_kernel, even=even, has_prev=prev is not None),
        grid=(b, ni),
        in_specs=in_specs,
        out_specs=out_specs,
        out_shape=out_shape,
        compiler_params=pltpu.CompilerParams(
            dimension_semantics=("parallel", "parallel"), vmem_limit_bytes=VMEM_LIMIT),
        name=("proj_even" if even else "proj_odd") + ("_fused" if prev is not None else ""),
    )(*args)


def _outproj_kernel(x_ref, y1_ref, y2_ref, w_ref, fg_ref, o_ref):
    y = jnp.concatenate([y1_ref[...], y2_ref[...]], axis=1)
    xn = x_ref[...] + jnp.dot(y, w_ref[...], preferred_element_type=jnp.float32)
    ms = jnp.mean(xn * xn, axis=-1, keepdims=True)
    o_ref[...] = (xn * lax.rsqrt(ms + RMS_EPS)) * fg_ref[...]


def _outproj_final(x, y1, y2, w_b, layer, fg):
    b, s, d = x.shape
    tm = min(PROJ_TM, s)
    return pl.pallas_call(
        _outproj_kernel,
        grid=(b, s // tm),
        in_specs=[
            pl.BlockSpec((None, tm, d), lambda bi, i: (bi, i, 0)),
            pl.BlockSpec((None, tm, BRANCH_W), lambda bi, i: (bi, i, 0)),
            pl.BlockSpec((None, tm, BRANCH_W), lambda bi, i: (bi, i, 0)),
            pl.BlockSpec((None, 2 * BRANCH_W, d), lambda bi, i: (layer, 0, 0)),
            pl.BlockSpec((1, d), lambda bi, i: (0, 0)),
        ],
        out_specs=pl.BlockSpec((None, tm, d), lambda bi, i: (bi, i, 0)),
        out_shape=jax.ShapeDtypeStruct((b, s, d), jnp.float32),
        compiler_params=pltpu.CompilerParams(
            dimension_semantics=("parallel", "parallel"), vmem_limit_bytes=VMEM_LIMIT),
        name="outproj_final",
    )(x, y1, y2, w_b, fg)


def _dilated_kernel(q_ref, kc_ref, kp_ref, vc_ref, vp_ref, g_ref, o_ref,
                    kwin, vwin, m_sc, l_sc, acc_sc):
    sb = q_ref.shape[0]
    first_super = pl.program_id(2) == 0
    kwin[0:sb, :] = kp_ref[...]
    kwin[sb:2 * sb, :] = kc_ref[...]
    vwin[0:sb, :] = vp_ref[...]
    vwin[sb:2 * sb, :] = vc_ref[...]

    t = DIL_STEPS
    lane = _lane_iota((t, LANES))
    row = _row_iota((t, 2 * t))
    col = _lane_iota((t, 2 * t))
    band = (col >= row) & (col <= row + t)

    for ri, r in enumerate(DIL_RATES):
        n_tiles = sb // t

        def tile(j, carry, r=r, ri=ri):
            phase = j % r
            nb = j // r
            q0 = phase + r * t * nb
            rows = pl.ds(q0, t, stride=r) if r > 1 else pl.ds(q0, t)
            own = pl.ds(sb + q0, t, stride=r) if r > 1 else pl.ds(sb + q0, t)
            prev = pl.ds(sb + q0 - r * t, t, stride=r) if r > 1 else pl.ds(sb + q0 - r * t, t)
            qt = q_ref[rows, :]
            kk = jnp.concatenate([kwin[prev, :], kwin[own, :]], axis=0).astype(jnp.bfloat16)
            vv = jnp.concatenate([vwin[prev, :], vwin[own, :]], axis=0).astype(jnp.bfloat16)
            no_prev = jnp.logical_and(first_super, nb == 0)
            mask = band & (col >= jnp.where(no_prev, t, 0))
            ms, ls, accs = [], [], []
            for h in range(HEADS_PER_BLOCK):
                hm = (lane // HEAD_DIM) == h
                qh = jnp.where(hm, qt, 0.0).astype(jnp.bfloat16)
                s = jnp.where(mask, _nt_dot(qh, kk), NEG)
                m = jnp.max(s, axis=1, keepdims=True)
                p = jnp.exp2(s - m)
                ms.append(m)
                ls.append(jnp.sum(p, axis=1, keepdims=True))
                accs.append(jnp.dot(p.astype(jnp.bfloat16), vv, preferred_element_type=jnp.float32))
            lo = lane < HEAD_DIM
            m_sc[ri, rows, :] = jnp.where(lo, ms[0], ms[1])
            l_sc[ri, rows, :] = jnp.where(lo, ls[0], ls[1])
            acc_sc[ri, rows, :] = jnp.where(lo, accs[0], accs[1])
            return carry

        lax.fori_loop(0, n_tiles, tile, 0, unroll=8)

    chunk = 256
    def combine(c, carry):
        rows = pl.ds(pl.multiple_of(c * chunk, chunk), chunk)
        m0, m1, m2 = m_sc[0, rows, :], m_sc[1, rows, :], m_sc[2, rows, :]
        mx = jnp.maximum(jnp.maximum(m0, m1), m2)
        w0, w1, w2 = jnp.exp2(m0 - mx), jnp.exp2(m1 - mx), jnp.exp2(m2 - mx)
        num = w0 * acc_sc[0, rows, :] + w1 * acc_sc[1, rows, :] + w2 * acc_sc[2, rows, :]
        den = w0 * l_sc[0, rows, :] + w1 * l_sc[1, rows, :] + w2 * l_sc[2, rows, :]
        o_ref[rows, :] = ((num / den) * _silu(g_ref[rows, :])).astype(o_ref.dtype)
        return carry
    lax.fori_loop(0, sb // chunk, combine, 0)


def _dilated(pf):
    b, s, _ = pf.shape
    sb = DIL_SUPER
    assert s % sb == 0
    blk = lambda part, prev: pl.BlockSpec(
        (None, sb, LANES),
        (lambda bi, hp, i: (bi, jnp.maximum(i - 1, 0), part * BLOCKS_PER_PART + hp)) if prev
        else (lambda bi, hp, i: (bi, i, part * BLOCKS_PER_PART + hp)))
    return pl.pallas_call(
        _dilated_kernel,
        grid=(b, BLOCKS_PER_PART, s // sb),
        in_specs=[blk(0, False), blk(1, False), blk(1, True), blk(2, False), blk(2, True),
                  blk(3, False)],
        out_specs=pl.BlockSpec((None, sb, LANES), lambda bi, hp, i: (bi, i, hp)),
        out_shape=jax.ShapeDtypeStruct((b, s, BRANCH_W), jnp.bfloat16),
        scratch_shapes=[
            pltpu.VMEM((2 * sb, LANES), jnp.float32),
            pltpu.VMEM((2 * sb, LANES), jnp.float32),
            pltpu.VMEM((len(DIL_RATES), sb, LANES), jnp.float32),
            pltpu.VMEM((len(DIL_RATES), sb, LANES), jnp.float32),
            pltpu.VMEM((len(DIL_RATES), sb, LANES), jnp.float32),
        ],
        compiler_params=pltpu.CompilerParams(
            dimension_semantics=("parallel", "parallel", "arbitrary"),
            vmem_limit_bytes=VMEM_LIMIT),
        name="dilated",
    )(pf, pf, pf, pf, pf, pf)


def _softmax_init(s, v, m_sc, l_sc, acc_sc, idx, rows):
    m = jnp.max(s, axis=1, keepdims=True)
    p = jnp.exp2(s - m)
    stat = (s.shape[0], LANES)
    m_sc[idx, rows, :] = jnp.broadcast_to(m, stat)
    l_sc[idx, rows, :] = jnp.broadcast_to(jnp.sum(p, axis=1, keepdims=True), stat)
    acc_sc[idx, rows, :] = jnp.dot(p.astype(jnp.bfloat16), v, preferred_element_type=jnp.float32)


def _diagonal_init(q_sc, n, ka, vb, m_sc, l_sc, acc_sc):
    tq = ka.shape[0]
    half = tq // 2
    for r in range(2):
        rows = pl.ds(r * half, half)
        ncols = (r + 1) * half
        causal = _lane_iota((half, ncols)) <= _row_iota((half, ncols)) + r * half
        for c in range(n):
            s = jnp.where(causal, _nt_dot(q_sc[c, rows, :], ka[:ncols]), NEG)
            _softmax_init(s, vb[:ncols], m_sc, l_sc, acc_sc, c, rows)


def _softmax_update(s, v, m_sc, l_sc, acc_sc, idx):
    m_prev = m_sc[idx]
    m_new = jnp.maximum(m_prev, jnp.max(s, axis=1, keepdims=True))
    alpha = jnp.exp2(m_prev - m_new)
    p = jnp.exp2(s - _rep(m_new, s.shape[1] // LANES))
    m_sc[idx] = m_new
    l_sc[idx] = alpha * l_sc[idx] + jnp.sum(p, axis=1, keepdims=True)
    acc_sc[idx] = alpha * acc_sc[idx] + jnp.dot(
        p.astype(jnp.bfloat16), v, preferred_element_type=jnp.float32)


def _moba_kernel(q_ref, k_ref, v_ref, e_ref, km_ref, g_ref, o_ref,
                 qa_sc, m_sc, l_sc, acc_sc):
    tq = q_ref.shape[0]
    i = pl.program_id(2)
    lane = _lane_iota((tq, LANES))
    blk = _row_iota((LANES, tq))
    cur = i * (tq // MOBA_BLOCK) + _lane_iota((LANES, tq)) // MOBA_BLOCK
    q = q_ref[...]
    km = km_ref[...]
    for h in range(HEADS_PER_BLOCK):
        hm = (lane // HEAD_DIM) == h
        qh = jnp.where(hm, q, jnp.zeros_like(q))
        gate = _nt_dot(km, qh.astype(jnp.float32), precision=lax.Precision.HIGHEST)
        gate = jnp.where(blk < cur, gate, -jnp.inf)
        sel = blk == cur
        for _ in range(MOBA_TOPK):
            mx = jnp.max(gate, axis=0, keepdims=True)
            first = jnp.min(jnp.where(gate == mx, blk, LANES), axis=0, keepdims=True)
            pick = (blk == first) & (mx > -jnp.inf)
            sel = sel | pick
            gate = jnp.where(pick, -jnp.inf, gate)
        bias = jnp.where(sel, 0.0, NEG).T.astype(jnp.bfloat16)
        qa_sc[h] = jnp.concatenate([qh, bias], axis=1)

    def kv(n):
        rows = pl.ds(pl.multiple_of(n * tq, tq), tq)
        return jnp.concatenate([k_ref[rows, :], e_ref[rows, :]], axis=1), v_ref[rows, :]

    ka, vb = kv(i)
    _diagonal_init(qa_sc, HEADS_PER_BLOCK, ka, vb, m_sc, l_sc, acc_sc)

    def body(n, carry):
        ka, vb = kv(n)
        for h in range(HEADS_PER_BLOCK):
            _softmax_update(_nt_dot(qa_sc[h], ka), vb, m_sc, l_sc, acc_sc, h)
        return carry
    lax.fori_loop(0, i, body, 0)

    o = jnp.where(lane < HEAD_DIM, acc_sc[0] / l_sc[0], acc_sc[1] / l_sc[1])
    o_ref[...] = (o * _silu(g_ref[...])).astype(o_ref.dtype)


def _moba(pb, pf, km, e):
    b, s, _ = pb.shape
    tq = min(ATT_TQ, s)
    assert s % tq == 0 and tq % MOBA_BLOCK == 0 and s // MOBA_BLOCK <= LANES
    col = lambda part: (lambda bi, hp, i: (bi, i, part * BLOCKS_PER_PART + hp))
    whole = lambda part: (lambda bi, hp, i: (bi, 0, part * BLOCKS_PER_PART + hp))
    return pl.pallas_call(
        _moba_kernel,
        grid=(b, BLOCKS_PER_PART, s // tq),
        in_specs=[
            pl.BlockSpec((None, tq, LANES), col(4)),
            pl.BlockSpec((None, s, LANES), whole(5)),
            pl.BlockSpec((None, s, LANES), whole(6)),
            pl.BlockSpec((s, LANES), lambda bi, hp, i: (0, 0)),
            pl.BlockSpec((None, LANES, LANES), lambda bi, hp, i: (bi, 0, hp)),
            pl.BlockSpec((None, tq, LANES), col(F32_PARTS_EVEN.index(7))),
        ],
        out_specs=pl.BlockSpec((None, tq, LANES), lambda bi, hp, i: (bi, i, hp)),
        out_shape=jax.ShapeDtypeStruct((b, s, BRANCH_W), jnp.bfloat16),
        scratch_shapes=[
            pltpu.VMEM((HEADS_PER_BLOCK, tq, 2 * LANES), jnp.bfloat16),
            pltpu.VMEM((HEADS_PER_BLOCK, tq, LANES), jnp.float32),
            pltpu.VMEM((HEADS_PER_BLOCK, tq, LANES), jnp.float32),
            pltpu.VMEM((HEADS_PER_BLOCK, tq, LANES), jnp.float32),
        ],
        compiler_params=pltpu.CompilerParams(
            dimension_semantics=("parallel", "parallel", "arbitrary"),
            vmem_limit_bytes=VMEM_LIMIT),
        name="moba",
    )(pb, pb, pb, e, km, pf)


def _diff_kernel(q_ref, k_ref, v_ref, lam_ref, sg_ref, g_ref, o_ref,
                 qm_sc, m_sc, l_sc, acc_sc, *, lambda_init):
    tq = q_ref.shape[0]
    i = pl.program_id(2)
    lane = _lane_iota((tq, LANES))
    q = q_ref[...]
    n_maps = LANES // DIFF_HEAD_DIM
    for c in range(n_maps):
        qm_sc[c] = jnp.where((lane // DIFF_HEAD_DIM) == c, q, jnp.zeros_like(q))

    def kv(n):
        rows = pl.ds(pl.multiple_of(n * ATT_TK, ATT_TK), ATT_TK)
        return k_ref[rows, :], v_ref[rows, :]

    kb, vb = kv(i)
    _diagonal_init(qm_sc, n_maps, kb, vb, m_sc, l_sc, acc_sc)

    def body(n, carry):
        kb, vb = kv(n)
        for c in range(n_maps):
            _softmax_update(_nt_dot(qm_sc[c], kb), vb, m_sc, l_sc, acc_sc, c)
        return carry
    lax.fori_loop(0, i, body, 0)

    lf = lam_ref[...]
    lam = (jnp.exp(jnp.sum(lf[0:1] * lf[1:2], axis=1, keepdims=True))
           - jnp.exp(jnp.sum(lf[2:3] * lf[3:4], axis=1, keepdims=True)) + lambda_init)
    o0 = acc_sc[0] / l_sc[0] - lam * (acc_sc[1] / l_sc[1])
    o1 = acc_sc[2] / l_sc[2] - lam * (acc_sc[3] / l_sc[3])
    lo = lane < HEAD_DIM
    o = jnp.where(lo, o0, o1)
    sq = o * o
    ms0 = jnp.sum(jnp.where(lo, sq, 0.0), axis=1, keepdims=True) / HEAD_DIM
    ms1 = jnp.sum(jnp.where(lo, 0.0, sq), axis=1, keepdims=True) / HEAD_DIM
    y = (o * lax.rsqrt(jnp.where(lo, ms0, ms1) + RMS_EPS)) * sg_ref[...]
    y = y * (1.0 - lambda_init)
    o_ref[...] = (y * _silu(g_ref[...])).astype(o_ref.dtype)


def _diff(pb, pf, lam, sg, lambda_init):
    b, s, _ = pb.shape
    tq = ATT_TQ
    assert tq == ATT_TK and s % tq == 0
    col = lambda part: (lambda bi, hp, i: (bi, i, part * BLOCKS_PER_PART + hp))
    whole = lambda part: (lambda bi, hp, i: (bi, 0, part * BLOCKS_PER_PART + hp))
    n_maps = LANES // DIFF_HEAD_DIM
    return pl.pallas_call(
        functools.partial(_diff_kernel, lambda_init=lambda_init),
        grid=(b, BLOCKS_PER_PART, s // tq),
        in_specs=[
            pl.BlockSpec((None, tq, LANES), col(0)),
            pl.BlockSpec((None, s, LANES), whole(1)),
            pl.BlockSpec((None, s, LANES), whole(2)),
            pl.BlockSpec((4, LANES), lambda bi, hp, i: (0, 0)),
            pl.BlockSpec((1, LANES), lambda bi, hp, i: (0, 0)),
            pl.BlockSpec((None, tq, LANES), col(F32_PARTS_ODD.index(3))),
        ],
        out_specs=pl.BlockSpec((None, tq, LANES), lambda bi, hp, i: (bi, i, hp)),
        out_shape=jax.ShapeDtypeStruct((b, s, BRANCH_W), jnp.bfloat16),
        scratch_shapes=[
            pltpu.VMEM((n_maps, tq, LANES), jnp.bfloat16),
            pltpu.VMEM((n_maps, tq, LANES), jnp.float32),
            pltpu.VMEM((n_maps, tq, LANES), jnp.float32),
            pltpu.VMEM((n_maps, tq, LANES), jnp.float32),
        ],
        compiler_params=pltpu.CompilerParams(
            dimension_semantics=("parallel", "parallel", "arbitrary"),
            vmem_limit_bytes=VMEM_LIMIT),
        name="diff",
    )(pb, pb, pb, lam, sg, pf)


def _stick_kernel(q_ref, k_ref, v_ref, tri_ref, g_ref, o_ref, qh_sc, run_sc, acc_sc):
    tq = q_ref.shape[0]
    i = pl.program_id(2)
    lane = _lane_iota((tq, LANES))
    q = q_ref[...]
    for h in range(HEADS_PER_BLOCK):
        qh_sc[h] = jnp.where((lane // HEAD_DIM) == h, q, jnp.zeros_like(q))
    tri = tri_ref[...]
    n_sub = tq // TRI_BLOCK
    sign_bit = jnp.uint32(0x80000000)

    def step(n, h, before):
        rows = pl.ds(pl.multiple_of(n * tq, tq), tq)
        z = _nt_dot(qh_sc[h], k_ref[rows, :])
        neg_abs = lax.bitcast_convert_type(
            lax.bitcast_convert_type(z, jnp.uint32) | sign_bit, jnp.float32)
        sp = jnp.maximum(z, 0.0) + jnp.log2(1.0 + jnp.exp2(neg_abs))
        if before is not None:
            sp = jnp.where(before, sp, 0.0)
        run = None if before is not None else run_sc[h]
        a_parts = [None] * n_sub
        for j in reversed(range(n_sub)):
            cols = slice(j * TRI_BLOCK, (j + 1) * TRI_BLOCK)
            sj = sp[:, cols]
            log_a = z[:, cols] - sj - jnp.dot(
                sj.astype(jnp.bfloat16), tri, preferred_element_type=jnp.float32)
            if run is not None:
                log_a = log_a - _rep(run, TRI_BLOCK // LANES)
            if before is not None:
                log_a = jnp.where(before[:, cols], log_a, NEG)
            a_parts[j] = jnp.exp2(log_a).astype(jnp.bfloat16)
            rowsum = jnp.sum(sj, axis=1, keepdims=True)
            run = jnp.broadcast_to(rowsum, run_sc.shape[1:]) if run is None else run + rowsum
        pv = jnp.dot(jnp.concatenate(a_parts, axis=1), v_ref[rows, :],
                     preferred_element_type=jnp.float32)
        run_sc[h] = run
        acc_sc[h] = pv if before is not None else acc_sc[h] + pv

    strictly_before = _lane_iota((tq, tq)) < _row_iota((tq, tq))
    for h in range(HEADS_PER_BLOCK):
        step(i, h, strictly_before)

    def alive():
        return jnp.min(jnp.minimum(run_sc[0], run_sc[1])) < STICK_DEAD_BITS

    def cond(carry):
        t, live = carry
        return jnp.logical_and(t < i, live)

    def body(carry):
        t, _ = carry
        for h in range(HEADS_PER_BLOCK):
            step(i - 1 - t, h, None)
        return t + 1, alive()

    lax.while_loop(cond, body, (jnp.int32(0), alive()))

    o = jnp.where(lane < HEAD_DIM, acc_sc[0], acc_sc[1])
    o_ref[...] = (o * _silu(g_ref[...])).astype(o_ref.dtype)


def _stick(pb, pf, tri):
    b, s, _ = pb.shape
    tq = min(STICK_T, s)
    assert s % tq == 0 and tq % TRI_BLOCK == 0
    col = lambda part: (lambda bi, hp, i: (bi, i, part * BLOCKS_PER_PART + hp))
    whole = lambda part: (lambda bi, hp, i: (bi, 0, part * BLOCKS_PER_PART + hp))
    return pl.pallas_call(
        _stick_kernel,
        grid=(b, BLOCKS_PER_PART, s // tq),
        in_specs=[
            pl.BlockSpec((None, tq, LANES), col(4)),
            pl.BlockSpec((None, s, LANES), whole(5)),
            pl.BlockSpec((None, s, LANES), whole(6)),
            pl.BlockSpec((TRI_BLOCK, TRI_BLOCK), lambda bi, hp, i: (0, 0)),
            pl.BlockSpec((None, tq, LANES), col(F32_PARTS_ODD.index(7))),
        ],
        out_specs=pl.BlockSpec((None, tq, LANES), lambda bi, hp, i: (bi, i, hp)),
        out_shape=jax.ShapeDtypeStruct((b, s, BRANCH_W), jnp.bfloat16),
        scratch_shapes=[
            pltpu.VMEM((HEADS_PER_BLOCK, tq, LANES), jnp.bfloat16),
            pltpu.VMEM((HEADS_PER_BLOCK, tq, LANES), jnp.float32),
            pltpu.VMEM((HEADS_PER_BLOCK, tq, LANES), jnp.float32),
        ],
        compiler_params=pltpu.CompilerParams(
            dimension_semantics=("parallel", "parallel", "arbitrary"),
            vmem_limit_bytes=VMEM_LIMIT),
        name="stick",
    )(pb, pb, pb, tri, pf)


def _rope_tables(s, dim):
    half = dim // 2
    inv = ROPE_THETA ** (-jnp.arange(half, dtype=jnp.float32) / half)
    ang = jnp.arange(s).astype(jnp.float32)[:, None] * inv[None, :]
    cos = jnp.tile(jnp.cos(ang), (1, LANES // half))
    sin = jnp.tile(jnp.sin(ang), (1, LANES // half))
    sign = jnp.where((jnp.arange(LANES) % dim) < half, -1.0, 1.0).astype(jnp.float32)
    return cos, sin * sign[None, :]


def kernel(x, norm_g, w_in, w_out, diff_lam, diff_subln_g, final_norm_g):
    b, s, d = x.shape
    depth = w_in.shape[0]
    w_in_b = w_in.astype(jnp.bfloat16)
    w_out_b = w_out.astype(jnp.bfloat16)
    norm_g3 = norm_g.reshape(depth, 1, d)
    fg = final_norm_g.reshape(1, d)
    rope_even = _rope_tables(s, HEAD_DIM)
    rope_odd = _rope_tables(s, DIFF_HEAD_DIM)
    n_moba = s // MOBA_BLOCK
    block_code = (jnp.arange(s)[:, None] // MOBA_BLOCK == jnp.arange(LANES)[None, :]
                  ).astype(jnp.bfloat16)
    tri = (jnp.arange(TRI_BLOCK)[:, None] > jnp.arange(TRI_BLOCK)[None, :]).astype(jnp.bfloat16)

    prev = None
    for layer in range(depth):
        even = layer % 2 == 0
        outs = _proj(x, norm_g3, w_in_b, layer, *(rope_even if even else rope_odd), even=even,
                     prev=prev)
        if prev is not None:
            x, outs = outs[0], outs[1:]
        if even:
            pf, pb, km = outs
            km = km.reshape(b, n_moba, BRANCH_W)
            km = jnp.pad(km, ((0, 0), (0, LANES - n_moba), (0, 0)))
            y1 = _dilated(pf)
            y2 = _moba(pb, pf, km, block_code)
        else:
            pf, pb = outs
            li = layer // 2
            lambda_init = 0.8 - 0.6 * math.exp(-0.3 * layer)
            lam = jnp.pad(diff_lam[li], ((0, 0), (0, LANES - DIFF_HEAD_DIM)))
            sg = jnp.tile(diff_subln_g[li], HEADS_PER_BLOCK).reshape(1, LANES)
            y1 = _diff(pb, pf, lam, sg, lambda_init)
            y2 = _stick(pb, pf, tri)
        prev = (y1, y2, w_out_b)
    return _outproj_final(x, y1, y2, w_out_b, depth - 1, fg)
```

```python
import functools
import math

import jax
import jax.numpy as jnp
from jax import lax
from jax.experimental import pallas as pl
from jax.experimental.pallas import tpu as pltpu

HEAD_DIM = 64
N_HEADS = 4
BRANCH_W = N_HEADS * HEAD_DIM
N_IN_PARTS = 8
PROJ_W = N_IN_PARTS * BRANCH_W
LANES = 128
HEADS_PER_BLOCK = LANES // HEAD_DIM
BLOCKS_PER_PART = BRANCH_W // LANES
ROPE_THETA = 10000.0
DIL_RATES = (1, 4, 16)
DIL_STEPS = 128
DIL_SUPER = DIL_STEPS * max(DIL_RATES)
MOBA_BLOCK = 256
MOBA_TOPK = 3
DIFF_HEAD_DIM = HEAD_DIM // 2
RMS_EPS = 1e-6
NEG = -1e30
LOG2E = math.log2(math.e)

ATT_TQ = 1024
ATT_TK = 1024
STICK_T = 512
STICK_DEAD_BITS = 160.0
TRI_BLOCK = 256
PROJ_TM = 512
F32_PARTS_EVEN = (0, 1, 2, 3, 7)
F32_PARTS_ODD = (3, 7)
VMEM_LIMIT = 56 * 1024 * 1024

_NT = (((1,), (1,)), ((), ()))


def _nt_dot(a, b, precision=None):
    return lax.dot_general(a, b, _NT, preferred_element_type=jnp.float32, precision=precision)


def _lane_iota(shape):
    return lax.broadcasted_iota(jnp.int32, shape, 1)


def _row_iota(shape):
    return lax.broadcasted_iota(jnp.int32, shape, 0)


def _silu(g):
    return g * (1.0 / (1.0 + jnp.exp(-g)))


def _rep(a, n):
    return a if n == 1 else jnp.concatenate([a] * n, axis=1)


def _proj_kernel(*refs, even, has_prev):
    if has_prev:
        (x_ref, y1_ref, y2_ref, wo_ref, g_ref, w_ref, cos_ref, sin_ref,
         xn_ref, pf_ref, pb_ref, *rest) = refs
        y = jnp.concatenate([y1_ref[...], y2_ref[...]], axis=1)
        x = x_ref[...] + jnp.dot(y, wo_ref[...], preferred_element_type=jnp.float32)
        xn_ref[...] = x
    else:
        x_ref, g_ref, w_ref, cos_ref, sin_ref, pf_ref, pb_ref, *rest = refs
        x = x_ref[...]
    ms = jnp.mean(x * x, axis=-1, keepdims=True)
    h = (x * lax.rsqrt(ms + RMS_EPS)) * g_ref[...]
    proj = jnp.dot(h.astype(jnp.bfloat16), w_ref[...], preferred_element_type=jnp.float32)

    if even:
        rope_parts, shift, f32_parts = (0, 1, 4, 5), HEAD_DIM // 2, F32_PARTS_EVEN
        scales = {0: LOG2E / math.sqrt(HEAD_DIM), 4: LOG2E / math.sqrt(HEAD_DIM)}
    else:
        rope_parts, shift, f32_parts = (0, 1), DIFF_HEAD_DIM // 2, F32_PARTS_ODD
        scales = {0: LOG2E / math.sqrt(DIFF_HEAD_DIM), 4: LOG2E / math.sqrt(HEAD_DIM)}
    cos = cos_ref[...]
    sin = sin_ref[...]
    first_half = (_lane_iota(cos.shape) % (2 * shift)) < shift

    for c in range(PROJ_W // LANES):
        part = c // BLOCKS_PER_PART
        t = proj[:, c * LANES:(c + 1) * LANES]
        if part in rope_parts:
            up = pltpu.roll(t, LANES - shift, axis=1)
            down = pltpu.roll(t, shift, axis=1)
            t = t * cos + jnp.where(first_half, up, down) * sin
        if part in scales:
            t = t * scales[part]
        if part in f32_parts:
            cf = f32_parts.index(part) * BLOCKS_PER_PART + c % BLOCKS_PER_PART
            pf_ref[:, cf * LANES:(cf + 1) * LANES] = t
        pb_ref[:, c * LANES:(c + 1) * LANES] = t.astype(jnp.bfloat16)
        if even and part == 5:
            km_ref = rest[0]
            cb = c - 5 * BLOCKS_PER_PART
            nblk = t.shape[0] // MOBA_BLOCK
            km_ref[:, cb * LANES:(cb + 1) * LANES] = jnp.mean(
                t.reshape(nblk, MOBA_BLOCK, LANES), axis=1)


def _proj(x, g, w_b, layer, cos, sin, even, prev=None):
    b, s, d = x.shape
    tm = min(PROJ_TM, s)
    assert s % tm == 0 and tm % MOBA_BLOCK == 0
    ni = s // tm
    f32_w = len(F32_PARTS_EVEN if even else F32_PARTS_ODD) * BRANCH_W
    row = lambda w: pl.BlockSpec((None, tm, w), lambda bi, i: (bi, i, 0))
    const = lambda shape, idx: pl.BlockSpec(shape, lambda bi, i: idx, pipeline_mode=pl.Buffered(1))
    out_shape = [jax.ShapeDtypeStruct((b, s, f32_w), jnp.float32),
                 jax.ShapeDtypeStruct((b, s, PROJ_W), jnp.bfloat16)]
    out_specs = [row(f32_w), row(PROJ_W)]
    if even:
        out_shape.append(jax.ShapeDtypeStruct((b, ni, tm // MOBA_BLOCK, BRANCH_W), jnp.float32))
        out_specs.append(pl.BlockSpec((None, None, tm // MOBA_BLOCK, BRANCH_W),
                                      lambda bi, i: (bi, i, 0, 0)))
    in_specs = [row(d)]
    args = [x]
    if prev is not None:
        y1, y2, w_out_b = prev
        in_specs += [row(BRANCH_W), row(BRANCH_W),
                     const((None, 2 * BRANCH_W, d), (layer - 1, 0, 0))]
        args += [y1, y2, w_out_b]
        out_shape.insert(0, jax.ShapeDtypeStruct((b, s, d), jnp.float32))
        out_specs.insert(0, row(d))
    in_specs += [const((None, 1, d), (layer, 0, 0)), const((None, d, PROJ_W), (layer, 0, 0)),
                 pl.BlockSpec((tm, LANES), lambda bi, i: (i, 0)),
                 pl.BlockSpec((tm, LANES), lambda bi, i: (i, 0))]
    args += [g, w_b, cos, sin]
    return pl.pallas_call(
        functools.partial(_proj_kernel, even=even, has_prev=prev is not None),
        grid=(b, ni),
        in_specs=in_specs,
        out_specs=out_specs,
        out_shape=out_shape,
        compiler_params=pltpu.CompilerParams(
            dimension_semantics=("parallel", "parallel"), vmem_limit_bytes=VMEM_LIMIT),
        name=("proj_even" if even else "proj_odd") + ("_fused" if prev is not None else ""),
    )(*args)


def _outproj_kernel(x_ref, y1_ref, y2_ref, w_ref, fg_ref, o_ref):
    y = jnp.concatenate([y1_ref[...], y2_ref[...]], axis=1)
    xn = x_ref[...] + jnp.dot(y, w_ref[...], preferred_element_type=jnp.float32)
    ms = jnp.mean(xn * xn, axis=-1, keepdims=True)
    o_ref[...] = (xn * lax.rsqrt(ms + RMS_EPS)) * fg_ref[...]


def _outproj_final(x, y1, y2, w_b, layer, fg):
    b, s, d = x.shape
    tm = min(PROJ_TM, s)
    return pl.pallas_call(
        _outproj_kernel,
        grid=(b, s // tm),
        in_specs=[
            pl.BlockSpec((None, tm, d), lambda bi, i: (bi, i, 0)),
            pl.BlockSpec((None, tm, BRANCH_W), lambda bi, i: (bi, i, 0)),
            pl.BlockSpec((None, tm, BRANCH_W), lambda bi, i: (bi, i, 0)),
            pl.BlockSpec((None, 2 * BRANCH_W, d), lambda bi, i: (layer, 0, 0)),
            pl.BlockSpec((1, d), lambda bi, i: (0, 0)),
        ],
        out_specs=pl.BlockSpec((None, tm, d), lambda bi, i: (bi, i, 0)),
        out_shape=jax.ShapeDtypeStruct((b, s, d), jnp.float32),
        compiler_params=pltpu.CompilerParams(
            dimension_semantics=("parallel", "parallel"), vmem_limit_bytes=VMEM_LIMIT),
        name="outproj_final",
    )(x, y1, y2, w_b, fg)


def _dilated_kernel(q_ref, kc_ref, kp_ref, vc_ref, vp_ref, g_ref, o_ref,
                    kwin, vwin, m_sc, l_sc, acc_sc):
    sb = q_ref.shape[0]
    first_super = pl.program_id(2) == 0
    kwin[0:sb, :] = kp_ref[...]
    kwin[sb:2 * sb, :] = kc_ref[...]
    vwin[0:sb, :] = vp_ref[...]
    vwin[sb:2 * sb, :] = vc_ref[...]

    t = DIL_STEPS
    lane = _lane_iota((t, LANES))
    row = _row_iota((t, 2 * t))
    col = _lane_iota((t, 2 * t))
    band = (col >= row) & (col <= row + t)

    for ri, r in enumerate(DIL_RATES):
        n_tiles = sb // t

        def tile(j, carry, r=r, ri=ri):
            phase = j % r
            nb = j // r
            q0 = phase + r * t * nb
            rows = pl.ds(q0, t, stride=r) if r > 1 else pl.ds(q0, t)
            own = pl.ds(sb + q0, t, stride=r) if r > 1 else pl.ds(sb + q0, t)
            prev = pl.ds(sb + q0 - r * t, t, stride=r) if r > 1 else pl.ds(sb + q0 - r * t, t)
            qt = q_ref[rows, :]
            kk = jnp.concatenate([kwin[prev, :], kwin[own, :]], axis=0).astype(jnp.bfloat16)
            vv = jnp.concatenate([vwin[prev, :], vwin[own, :]], axis=0).astype(jnp.bfloat16)
            no_prev = jnp.logical_and(first_super, nb == 0)
            mask = band & (col >= jnp.where(no_prev, t, 0))
            ms, ls, accs = [], [], []
            for h in range(HEADS_PER_BLOCK):
                hm = (lane // HEAD_DIM) == h
                qh = jnp.where(hm, qt, 0.0).astype(jnp.bfloat16)
                s = jnp.where(mask, _nt_dot(qh, kk), NEG)
                m = jnp.max(s, axis=1, keepdims=True)
                p = jnp.exp2(s - m)
                ms.append(m)
                ls.append(jnp.sum(p, axis=1, keepdims=True))
                accs.append(jnp.dot(p.astype(jnp.bfloat16), vv, preferred_element_type=jnp.float32))
            lo = lane < HEAD_DIM
            m_sc[ri, rows, :] = jnp.where(lo, ms[0], ms[1])
            l_sc[ri, rows, :] = jnp.where(lo, ls[0], ls[1])
            acc_sc[ri, rows, :] = jnp.where(lo, accs[0], accs[1])
            return carry

        lax.fori_loop(0, n_tiles, tile, 0, unroll=8)

    chunk = 256
    def combine(c, carry):
        rows = pl.ds(pl.multiple_of(c * chunk, chunk), chunk)
        m0, m1, m2 = m_sc[0, rows, :], m_sc[1, rows, :], m_sc[2, rows, :]
        mx = jnp.maximum(jnp.maximum(m0, m1), m2)
        w0, w1, w2 = jnp.exp2(m0 - mx), jnp.exp2(m1 - mx), jnp.exp2(m2 - mx)
        num = w0 * acc_sc[0, rows, :] + w1 * acc_sc[1, rows, :] + w2 * acc_sc[2, rows, :]
        den = w0 * l_sc[0, rows, :] + w1 * l_sc[1, rows, :] + w2 * l_sc[2, rows, :]
        o_ref[rows, :] = ((num / den) * _silu(g_ref[rows, :])).astype(o_ref.dtype)
        return carry
    lax.fori_loop(0, sb // chunk, combine, 0)


def _dilated(pf):
    b, s, _ = pf.shape
    sb = DIL_SUPER
    assert s % sb == 0
    blk = lambda part, prev: pl.BlockSpec(
        (None, sb, LANES),
        (lambda bi, hp, i: (bi, jnp.maximum(i - 1, 0), part * BLOCKS_PER_PART + hp)) if prev
        else (lambda bi, hp, i: (bi, i, part * BLOCKS_PER_PART + hp)))
    return pl.pallas_call(
        _dilated_kernel,
        grid=(b, BLOCKS_PER_PART, s // sb),
        in_specs=[blk(0, False), blk(1, False), blk(1, True), blk(2, False), blk(2, True),
                  blk(3, False)],
        out_specs=pl.BlockSpec((None, sb, LANES), lambda bi, hp, i: (bi, i, hp)),
        out_shape=jax.ShapeDtypeStruct((b, s, BRANCH_W), jnp.bfloat16),
        scratch_shapes=[
            pltpu.VMEM((2 * sb, LANES), jnp.float32),
            pltpu.VMEM((2 * sb, LANES), jnp.float32),
            pltpu.VMEM((len(DIL_RATES), sb, LANES), jnp.float32),
            pltpu.VMEM((len(DIL_RATES), sb, LANES), jnp.float32),
            pltpu.VMEM((len(DIL_RATES), sb, LANES), jnp.float32),
        ],
        compiler_params=pltpu.CompilerParams(
            dimension_semantics=("parallel", "parallel", "arbitrary"),
            vmem_limit_bytes=VMEM_LIMIT),
        name="dilated",
    )(pf, pf, pf, pf, pf, pf)


def _softmax_init(s, v, m_sc, l_sc, acc_sc, idx, rows):
    m = jnp.max(s, axis=1, keepdims=True)
    p = jnp.exp2(s - m)
    stat = (s.shape[0], LANES)
    m_sc[idx, rows, :] = jnp.broadcast_to(m, stat)
    l_sc[idx, rows, :] = jnp.broadcast_to(jnp.sum(p, axis=1, keepdims=True), stat)
    acc_sc[idx, rows, :] = jnp.dot(p.astype(jnp.bfloat16), v, preferred_element_type=jnp.float32)


def _diagonal_init(q_sc, n, ka, vb, m_sc, l_sc, acc_sc):
    tq = ka.shape[0]
    half = tq // 2
    for r in range(2):
        rows = pl.ds(r * half, half)
        ncols = (r + 1) * half
        causal = _lane_iota((half, ncols)) <= _row_iota((half, ncols)) + r * half
        for c in range(n):
            s = jnp.where(causal, _nt_dot(q_sc[c, rows, :], ka[:ncols]), NEG)
            _softmax_init(s, vb[:ncols], m_sc, l_sc, acc_sc, c, rows)


def _softmax_update(s, v, m_sc, l_sc, acc_sc, idx):
    m_prev = m_sc[idx]
    m_new = jnp.maximum(m_prev, jnp.max(s, axis=1, keepdims=True))
    alpha = jnp.exp2(m_prev - m_new)
    p = jnp.exp2(s - _rep(m_new, s.shape[1] // LANES))
    m_sc[idx] = m_new
    l_sc[idx] = alpha * l_sc[idx] + jnp.sum(p, axis=1, keepdims=True)
    acc_sc[idx] = alpha * acc_sc[idx] + jnp.dot(
        p.astype(jnp.bfloat16), v, preferred_element_type=jnp.float32)


def _moba_kernel(q_ref, k_ref, v_ref, e_ref, km_ref, g_ref, o_ref,
                 qa_sc, m_sc, l_sc, acc_sc):
    tq = q_ref.shape[0]
    i = pl.program_id(2)
    lane = _lane_iota((tq, LANES))
    blk = _row_iota((LANES, tq))
    cur = i * (tq // MOBA_BLOCK) + _lane_iota((LANES, tq)) // MOBA_BLOCK
    q = q_ref[...]
    km = km_ref[...]
    for h in range(HEADS_PER_BLOCK):
        hm = (lane // HEAD_DIM) == h
        qh = jnp.where(hm, q, jnp.zeros_like(q))
        gate = _nt_dot(km, qh.astype(jnp.float32), precision=lax.Precision.HIGHEST)
        gate = jnp.where(blk < cur, gate, -jnp.inf)
        sel = blk == cur
        for _ in range(MOBA_TOPK):
            mx = jnp.max(gate, axis=0, keepdims=True)
            first = jnp.min(jnp.where(gate == mx, blk, LANES), axis=0, keepdims=True)
            pick = (blk == first) & (mx > -jnp.inf)
            sel = sel | pick
            gate = jnp.where(pick, -jnp.inf, gate)
        bias = jnp.where(sel, 0.0, NEG).T.astype(jnp.bfloat16)
        qa_sc[h] = jnp.concatenate([qh, bias], axis=1)

    def kv(n):
        rows = pl.ds(pl.multiple_of(n * tq, tq), tq)
        return jnp.concatenate([k_ref[rows, :], e_ref[rows, :]], axis=1), v_ref[rows, :]

    ka, vb = kv(i)
    _diagonal_init(qa_sc, HEADS_PER_BLOCK, ka, vb, m_sc, l_sc, acc_sc)

    def body(n, carry):
        ka, vb = kv(n)
        for h in range(HEADS_PER_BLOCK):
            _softmax_update(_nt_dot(qa_sc[h], ka), vb, m_sc, l_sc, acc_sc, h)
        return carry
    lax.fori_loop(0, i, body, 0)

    o = jnp.where(lane < HEAD_DIM, acc_sc[0] / l_sc[0], acc_sc[1] / l_sc[1])
    o_ref[...] = (o * _silu(g_ref[...])).astype(o_ref.dtype)


def _moba(pb, pf, km, e):
    b, s, _ = pb.shape
    tq = min(ATT_TQ, s)
    assert s % tq == 0 and tq % MOBA_BLOCK == 0 and s // MOBA_BLOCK <= LANES
    col = lambda part: (lambda bi, hp, i: (bi, i, part * BLOCKS_PER_PART + hp))
    whole = lambda part: (lambda bi, hp, i: (bi, 0, part * BLOCKS_PER_PART + hp))
    return pl.pallas_call(
        _moba_kernel,
        grid=(b, BLOCKS_PER_PART, s // tq),
        in_specs=[
            pl.BlockSpec((None, tq, LANES), col(4)),
            pl.BlockSpec((None, s, LANES), whole(5)),
            pl.BlockSpec((None, s, LANES), whole(6)),
            pl.BlockSpec((s, LANES), lambda bi, hp, i: (0, 0)),
            pl.BlockSpec((None, LANES, LANES), lambda bi, hp, i: (bi, 0, hp)),
            pl.BlockSpec((None, tq, LANES), col(F32_PARTS_EVEN.index(7))),
        ],
        out_specs=pl.BlockSpec((None, tq, LANES), lambda bi, hp, i: (bi, i, hp)),
        out_shape=jax.ShapeDtypeStruct((b, s, BRANCH_W), jnp.bfloat16),
        scratch_shapes=[
            pltpu.VMEM((HEADS_PER_BLOCK, tq, 2 * LANES), jnp.bfloat16),
            pltpu.VMEM((HEADS_PER_BLOCK, tq, LANES), jnp.float32),
            pltpu.VMEM((HEADS_PER_BLOCK, tq, LANES), jnp.float32),
            pltpu.VMEM((HEADS_PER_BLOCK, tq, LANES), jnp.float32),
        ],
        compiler_params=pltpu.CompilerParams(
            dimension_semantics=("parallel", "parallel", "arbitrary"),
            vmem_limit_bytes=VMEM_LIMIT),
        name="moba",
    )(pb, pb, pb, e, km, pf)


def _diff_kernel(q_ref, k_ref, v_ref, lam_ref, sg_ref, g_ref, o_ref,
                 qm_sc, m_sc, l_sc, acc_sc, *, lambda_init):
    tq = q_ref.shape[0]
    i = pl.program_id(2)
    lane = _lane_iota((tq, LANES))
    q = q_ref[...]
    n_maps = LANES // DIFF_HEAD_DIM
    for c in range(n_maps):
        qm_sc[c] = jnp.where((lane // DIFF_HEAD_DIM) == c, q, jnp.zeros_like(q))

    def kv(n):
        rows = pl.ds(pl.multiple_of(n * ATT_TK, ATT_TK), ATT_TK)
        return k_ref[rows, :], v_ref[rows, :]

    kb, vb = kv(i)
    _diagonal_init(qm_sc, n_maps, kb, vb, m_sc, l_sc, acc_sc)

    def body(n, carry):
        kb, vb = kv(n)
        for c in range(n_maps):
            _softmax_update(_nt_dot(qm_sc[c], kb), vb, m_sc, l_sc, acc_sc, c)
        return carry
    lax.fori_loop(0, i, body, 0)

    lf = lam_ref[...]
    lam = (jnp.exp(jnp.sum(lf[0:1] * lf[1:2], axis=1, keepdims=True))
           - jnp.exp(jnp.sum(lf[2:3] * lf[3:4], axis=1, keepdims=True)) + lambda_init)
    o0 = acc_sc[0] / l_sc[0] - lam * (acc_sc[1] / l_sc[1])
    o1 = acc_sc[2] / l_sc[2] - lam * (acc_sc[3] / l_sc[3])
    lo = lane < HEAD_DIM
    o = jnp.where(lo, o0, o1)
    sq = o * o
    ms0 = jnp.sum(jnp.where(lo, sq, 0.0), axis=1, keepdims=True) / HEAD_DIM
    ms1 = jnp.sum(jnp.where(lo, 0.0, sq), axis=1, keepdims=True) / HEAD_DIM
    y = (o * lax.rsqrt(jnp.where(lo, ms0, ms1) + RMS_EPS)) * sg_ref[...]
    y = y * (1.0 - lambda_init)
    o_ref[...] = (y * _silu(g_ref[...])).astype(o_ref.dtype)


def _diff(pb, pf, lam, sg, lambda_init):
    b, s, _ = pb.shape
    tq = ATT_TQ
    assert tq == ATT_TK and s % tq == 0
    col = lambda part: (lambda bi, hp, i: (bi, i, part * BLOCKS_PER_PART + hp))
    whole = lambda part: (lambda bi, hp, i: (bi, 0, part * BLOCKS_PER_PART + hp))
    n_maps = LANES // DIFF_HEAD_DIM
    return pl.pallas_call(
        functools.partial(_diff_kernel, lambda_init=lambda_init),
        grid=(b, BLOCKS_PER_PART, s // tq),
        in_specs=[
            pl.BlockSpec((None, tq, LANES), col(0)),
            pl.BlockSpec((None, s, LANES), whole(1)),
            pl.BlockSpec((None, s, LANES), whole(2)),
            pl.BlockSpec((4, LANES), lambda bi, hp, i: (0, 0)),
            pl.BlockSpec((1, LANES), lambda bi, hp, i: (0, 0)),
            pl.BlockSpec((None, tq, LANES), col(F32_PARTS_ODD.index(3))),
        ],
        out_specs=pl.BlockSpec((None, tq, LANES), lambda bi, hp, i: (bi, i, hp)),
        out_shape=jax.ShapeDtypeStruct((b, s, BRANCH_W), jnp.bfloat16),
        scratch_shapes=[
            pltpu.VMEM((n_maps, tq, LANES), jnp.bfloat16),
            pltpu.VMEM((n_maps, tq, LANES), jnp.float32),
            pltpu.VMEM((n_maps, tq, LANES), jnp.float32),
            pltpu.VMEM((n_maps, tq, LANES), jnp.float32),
        ],
        compiler_params=pltpu.CompilerParams(
            dimension_semantics=("parallel", "parallel", "arbitrary"),
            vmem_limit_bytes=VMEM_LIMIT),
        name="diff",
    )(pb, pb, pb, lam, sg, pf)


def _stick_kernel(q_ref, k_ref, v_ref, tri_ref, g_ref, o_ref, qh_sc, run_sc, acc_sc):
    tq = q_ref.shape[0]
    i = pl.program_id(2)
    lane = _lane_iota((tq, LANES))
    q = q_ref[...]
    for h in range(HEADS_PER_BLOCK):
        qh_sc[h] = jnp.where((lane // HEAD_DIM) == h, q, jnp.zeros_like(q))
    tri = tri_ref[...]
    sign_bit = jnp.uint32(0x80000000)

    def step(start, width, h, before):
        rows = pl.ds(pl.multiple_of(start, TRI_BLOCK), width)
        z = _nt_dot(qh_sc[h], k_ref[rows, :])
        neg_abs = lax.bitcast_convert_type(
            lax.bitcast_convert_type(z, jnp.uint32) | sign_bit, jnp.float32)
        sp = jnp.maximum(z, 0.0) + jnp.log2(1.0 + jnp.exp2(neg_abs))
        if before is not None:
            sp = jnp.where(before, sp, 0.0)
        run = None if before is not None else run_sc[h]
        n_sub = width // TRI_BLOCK
        a_parts = [None] * n_sub
        for j in reversed(range(n_sub)):
            cols = slice(j * TRI_BLOCK, (j + 1) * TRI_BLOCK)
            sj = sp[:, cols]
            log_a = z[:, cols] - sj - jnp.dot(
                sj.astype(jnp.bfloat16), tri, preferred_element_type=jnp.float32)
            if run is not None:
                log_a = log_a - _rep(run, TRI_BLOCK // LANES)
            if before is not None:
                log_a = jnp.where(before[:, cols], log_a, NEG)
            a_parts[j] = jnp.exp2(log_a).astype(jnp.bfloat16)
            rowsum = jnp.sum(sj, axis=1, keepdims=True)
            run = jnp.broadcast_to(rowsum, run_sc.shape[1:]) if run is None else run + rowsum
        a = a_parts[0] if n_sub == 1 else jnp.concatenate(a_parts, axis=1)
        pv = jnp.dot(a, v_ref[rows, :], preferred_element_type=jnp.float32)
        run_sc[h] = run
        acc_sc[h] = pv if before is not None else acc_sc[h] + pv

    strictly_before = _lane_iota((tq, tq)) < _row_iota((tq, tq))
    for h in range(HEADS_PER_BLOCK):
        step(i * tq, tq, h, strictly_before)

    n_past = i * (tq // TRI_BLOCK)

    def alive():
        return jnp.min(jnp.minimum(run_sc[0], run_sc[1])) < STICK_DEAD_BITS

    def cond(carry):
        t, live = carry
        return jnp.logical_and(t < n_past, live)

    def body(carry):
        t, _ = carry
        for h in range(HEADS_PER_BLOCK):
            step((n_past - 1 - t) * TRI_BLOCK, TRI_BLOCK, h, None)
        return t + 1, alive()

    lax.while_loop(cond, body, (jnp.int32(0), alive()))

    o = jnp.where(lane < HEAD_DIM, acc_sc[0], acc_sc[1])
    o_ref[...] = (o * _silu(g_ref[...])).astype(o_ref.dtype)


def _stick(pb, pf, tri):
    b, s, _ = pb.shape
    tq = min(STICK_T, s)
    assert s % tq == 0 and tq % TRI_BLOCK == 0
    col = lambda part: (lambda bi, hp, i: (bi, i, part * BLOCKS_PER_PART + hp))
    whole = lambda part: (lambda bi, hp, i: (bi, 0, part * BLOCKS_PER_PART + hp))
    return pl.pallas_call(
        _stick_kernel,
        grid=(b, BLOCKS_PER_PART, s // tq),
        in_specs=[
            pl.BlockSpec((None, tq, LANES), col(4)),
            pl.BlockSpec((None, s, LANES), whole(5)),
            pl.BlockSpec((None, s, LANES), whole(6)),
            pl.BlockSpec((TRI_BLOCK, TRI_BLOCK), lambda bi, hp, i: (0, 0)),
            pl.BlockSpec((None, tq, LANES), col(F32_PARTS_ODD.index(7))),
        ],
        out_specs=pl.BlockSpec((None, tq, LANES), lambda bi, hp, i: (bi, i, hp)),
        out_shape=jax.ShapeDtypeStruct((b, s, BRANCH_W), jnp.bfloat16),
        scratch_shapes=[
            pltpu.VMEM((HEADS_PER_BLOCK, tq, LANES), jnp.bfloat16),
            pltpu.VMEM((HEADS_PER_BLOCK, tq, LANES), jnp.float32),
            pltpu.VMEM((HEADS_PER_BLOCK, tq, LANES), jnp.float32),
        ],
        compiler_params=pltpu.CompilerParams(
            dimension_semantics=("parallel", "parallel", "arbitrary"),
            vmem_limit_bytes=VMEM_LIMIT),
        name="stick",
    )(pb, pb, pb, tri, pf)


def _rope_tables(s, dim):
    half = dim // 2
    inv = ROPE_THETA ** (-jnp.arange(half, dtype=jnp.float32) / half)
    ang = jnp.arange(s).astype(jnp.float32)[:, None] * inv[None, :]
    cos = jnp.tile(jnp.cos(ang), (1, LANES // half))
    sin = jnp.tile(jnp.sin(ang), (1, LANES // half))
    sign = jnp.where((jnp.arange(LANES) % dim) < half, -1.0, 1.0).astype(jnp.float32)
    return cos, sin * sign[None, :]


def kernel(x, norm_g, w_in, w_out, diff_lam, diff_subln_g, final_norm_g):
    b, s, d = x.shape
    depth = w_in.shape[0]
    w_in_b = w_in.astype(jnp.bfloat16)
    w_out_b = w_out.astype(jnp.bfloat16)
    norm_g3 = norm_g.reshape(depth, 1, d)
    fg = final_norm_g.reshape(1, d)
    rope_even = _rope_tables(s, HEAD_DIM)
    rope_odd = _rope_tables(s, DIFF_HEAD_DIM)
    n_moba = s // MOBA_BLOCK
    block_code = (jnp.arange(s)[:, None] // MOBA_BLOCK == jnp.arange(LANES)[None, :]
                  ).astype(jnp.bfloat16)
    tri = (jnp.arange(TRI_BLOCK)[:, None] > jnp.arange(TRI_BLOCK)[None, :]).astype(jnp.bfloat16)

    prev = None
    for layer in range(depth):
        even = layer % 2 == 0
        outs = _proj(x, norm_g3, w_in_b, layer, *(rope_even if even else rope_odd), even=even,
                     prev=prev)
        if prev is not None:
            x, outs = outs[0], outs[1:]
        if even:
            pf, pb, km = outs
            km = km.reshape(b, n_moba, BRANCH_W)
            km = jnp.pad(km, ((0, 0), (0, LANES - n_moba), (0, 0)))
            y1 = _dilated(pf)
            y2 = _moba(pb, pf, km, block_code)
        else:
            pf, pb = outs
            li = layer // 2
            lambda_init = 0.8 - 0.6 * math.exp(-0.3 * layer)
            lam = jnp.pad(diff_lam[li], ((0, 0), (0, LANES - DIFF_HEAD_DIM)))
            sg = jnp.tile(diff_subln_g[li], HEADS_PER_BLOCK).reshape(1, LANES)
            y1 = _diff(pb, pf, lam, sg, lambda_init)
            y2 = _stick(pb, pf, tri)
        prev = (y1, y2, w_out_b)
    return _outproj_final(x, y1, y2, w_out_b, depth - 1, fg)
```
